```python
import math
import jax, jax.numpy as jnp
from jax import lax
import numpy as np

D_MODEL = 2048
BATCH = 16
SEQ = 2048
DEPTH = 2

D_MIX = D_MODEL
DA_WIDTH = D_MIX // 2
SSD_INNER = D_MIX - DA_WIDTH
DA_V_DIM = 128
DA_HEADS = DA_WIDTH // DA_V_DIM
DA_QK_DIM = DA_V_DIM // 2
ROT_DIM = DA_QK_DIM // 4
ROPE_THETA = 500000.0
Q_BLOCK = 128
SSD_HEAD_DIM = 64
SSD_HEADS = SSD_INNER // SSD_HEAD_DIM
SSD_GROUPS = 2
SSD_STATE = 128
CONV_WIDTH = 4
SSD_CHUNK = 128
XBC_DIM = SSD_INNER + 2 * SSD_GROUPS * SSD_STATE
IN_SPLITS = [DA_HEADS * 2 * DA_QK_DIM,
             DA_HEADS * 2 * DA_QK_DIM,
             DA_HEADS * DA_V_DIM,
             SSD_INNER,
             XBC_DIM,
             SSD_HEADS]
IN_DIM = sum(IN_SPLITS)
PEER_HEADS = 8
PEER_N_KEYS = 128
PEER_EXPERTS = PEER_N_KEYS * PEER_N_KEYS
PEER_HALF = 128
PEER_QUERY_DIM = 2 * PEER_HALF
PEER_TOPK = 16
PEER_TOKEN_BLOCK = 128
RMS_EPS = 1e-6

kernel_name = "hybrid_diffattn_ssd_peer_adaln"


def rms_norm(x, g, eps=RMS_EPS):
    xf = x.astype(jnp.float32)
    y = xf * lax.rsqrt(jnp.mean(xf * xf, axis=-1, keepdims=True) + eps)
    return (y * g.astype(jnp.float32)).astype(x.dtype)


def modulate(h, shift, scale):
    return h * (1.0 + scale[:, None, :]) + shift[:, None, :]


def apply_partial_rope(x, cos, sin):
    half = ROT_DIM // 2
    x1 = x[..., :half]
    x2 = x[..., half:ROT_DIM]
    return jnp.concatenate([x1 * cos - x2 * sin, x2 * cos + x1 * sin, x[..., ROT_DIM:]], axis=-1)


def diff_attention(q, k, v, lam):
    b, s, h, _, d = q.shape
    nblk = s // Q_BLOCK
    qb = jnp.moveaxis(q.reshape(b, nblk, Q_BLOCK, h, 2, d), 1, 0)
    key_pos = jnp.arange(s)
    scale = d ** -0.5

    def block(args):
        qi, i = args
        sc = jnp.einsum("bqhcd,bkhcd->bhcqk", qi, k).astype(jnp.float32) * scale
        q_pos = i * Q_BLOCK + jnp.arange(Q_BLOCK)
        mask = key_pos[None, :] <= q_pos[:, None]
        p = jax.nn.softmax(jnp.where(mask, sc, -jnp.inf), axis=-1)
        a = p[:, :, 0] - lam * p[:, :, 1]
        return jnp.einsum("bhqk,bkhe->bqhe", a.astype(v.dtype), v)

    out = lax.map(block, (qb, jnp.arange(nblk)))
    return jnp.moveaxis(out, 0, 1).reshape(b, s, h, v.shape[-1])


def causal_depthwise_conv(x, w, bias):
    ch = x.shape[-1]
    out = lax.conv_general_dilated(x, w[:, None, :], window_strides=(1,),
                                   padding=[(CONV_WIDTH - 1, 0)],
                                   dimension_numbers=("NWC", "WIO", "NWC"),
                                   feature_group_count=ch)
    return out + bias


def ssd_chunked(X, A, B, C):
    b, s, h, p = X.shape
    g, n = B.shape[2], B.shape[3]
    j = h // g
    nc = s // SSD_CHUNK
    X = X.reshape(b, nc, SSD_CHUNK, g, j, p)
    A = A.reshape(b, nc, SSD_CHUNK, g, j).transpose(0, 3, 4, 1, 2)
    B = B.reshape(b, nc, SSD_CHUNK, g, n)
    C = C.reshape(b, nc, SSD_CHUNK, g, n)
    A_cs = jnp.cumsum(A, axis=-1)
    causal = jnp.tril(jnp.ones((SSD_CHUNK, SSD_CHUNK), dtype=bool))
    seg = A_cs[..., :, None] - A_cs[..., None, :]
    L = jnp.exp(jnp.where(causal, seg, -jnp.inf))
    CB = jnp.einsum("bclgn,bcsgn->bcgls", C, B)
    y_diag = jnp.einsum("bcgls,bgjcls,bcsgjp->bclgjp", CB, L, X)
    decay_states = jnp.exp(A_cs[..., -1:] - A_cs)
    states = jnp.einsum("bclgn,bgjcl,bclgjp->bcgjpn", B, decay_states, X)
    chunk_decay = jnp.exp(A_cs[..., -1])

    def step(hst, inp):
        st, dec = inp
        return hst * dec[..., None, None] + st, hst

    init = jnp.zeros((b, g, j, p, n), dtype=states.dtype)
    _, prev = lax.scan(step, init, (jnp.moveaxis(states, 1, 0), jnp.moveaxis(chunk_decay, -1, 0)))
    prev = jnp.moveaxis(prev, 0, 1)
    y_off = jnp.einsum("bclgn,bcgjpn,bgjcl->bclgjp", C, prev, jnp.exp(A_cs))
    return (y_diag + y_off).reshape(b, s, h, p)


def ssd_group(z, xbc, dt_raw, conv_w, conv_b, dt_bias, a_log, d_skip, norm_g):
    b, s, _ = z.shape
    xbc = jax.nn.silu(causal_depthwise_conv(xbc, conv_w, conv_b))
    xs, bm, cm = jnp.split(xbc, [SSD_INNER, SSD_INNER + SSD_GROUPS * SSD_STATE], axis=-1)
    xs = xs.reshape(b, s, SSD_HEADS, SSD_HEAD_DIM)
    bm = bm.reshape(b, s, SSD_GROUPS, SSD_STATE).astype(jnp.float32)
    cm = cm.reshape(b, s, SSD_GROUPS, SSD_STATE).astype(jnp.float32)
    dt = jax.nn.softplus(dt_raw.astype(jnp.float32) + dt_bias.astype(jnp.float32))
    a = -jnp.exp(a_log.astype(jnp.float32))
    y = ssd_chunked(xs.astype(jnp.float32) * dt[..., None], dt * a, bm, cm)
    y = y + d_skip.astype(jnp.float32)[:, None] * xs.astype(jnp.float32)
    y = y.reshape(b, s, SSD_INNER) * jax.nn.silu(z.astype(jnp.float32))
    y = rms_norm(y.reshape(b, s, SSD_GROUPS, SSD_INNER // SSD_GROUPS),
                 norm_g.reshape(SSD_GROUPS, SSD_INNER // SSD_GROUPS))
    return y.reshape(b, s, SSD_INNER).astype(z.dtype)


def hybrid_mixer(h, cos, sin, lam_init, w_in, q_norm_g, k_norm_g, lam_q1, lam_k1, lam_q2, lam_k2,
                 subln_g, conv_w, conv_b, dt_bias, a_log, d_skip, ssd_norm_g, w_out):
    b, s, _ = h.shape
    proj = h @ w_in
    q, k, v, z, xbc, dt_raw = jnp.split(proj, list(np.cumsum(IN_SPLITS)[:-1]), axis=-1)
    q = q.reshape(b, s, DA_HEADS, 2, DA_QK_DIM)
    k = k.reshape(b, s, DA_HEADS, 2, DA_QK_DIM)
    v = v.reshape(b, s, DA_HEADS, DA_V_DIM)
    q = apply_partial_rope(rms_norm(q, q_norm_g), cos, sin)
    k = apply_partial_rope(rms_norm(k, k_norm_g), cos, sin)
    lam = (jnp.exp(jnp.sum(lam_q1.astype(jnp.float32) * lam_k1.astype(jnp.float32)))
           - jnp.exp(jnp.sum(lam_q2.astype(jnp.float32) * lam_k2.astype(jnp.float32))) + lam_init)
    attn = diff_attention(q, k, v, lam)
    attn = (rms_norm(attn, subln_g) * (1.0 - lam_init)).reshape(b, s, DA_WIDTH)
    ssd = ssd_group(z, xbc, dt_raw, conv_w, conv_b, dt_bias, a_log, d_skip, ssd_norm_g)
    return jnp.concatenate([attn, ssd], axis=-1) @ w_out


def peer_ffn(h, wq, k1, k2, u, v):
    b, s, d = h.shape
    t = b * s
    hf = h.reshape(t, d)
    q = (hf @ wq).reshape(t, PEER_HEADS, 2, PEER_HALF)
    s1 = jnp.einsum("thd,hkd->thk", q[:, :, 0], k1).astype(jnp.float32)
    s2 = jnp.einsum("thd,hkd->thk", q[:, :, 1], k2).astype(jnp.float32)
    v1, i1 = lax.top_k(s1, PEER_TOPK)
    v2, i2 = lax.top_k(s2, PEER_TOPK)
    cand = (v1[..., :, None] + v2[..., None, :]).reshape(t, PEER_HEADS, PEER_TOPK * PEER_TOPK)
    cand_idx = (i1[..., :, None] * PEER_N_KEYS + i2[..., None, :]).reshape(t, PEER_HEADS, PEER_TOPK * PEER_TOPK)
    top, pos = lax.top_k(cand, PEER_TOPK)
    idx = jnp.take_along_axis(cand_idx, pos, axis=-1)
    gate = jax.nn.softmax(top, axis=-1)
    nb = t // PEER_TOKEN_BLOCK
    idx = idx.reshape(nb, PEER_TOKEN_BLOCK, PEER_HEADS * PEER_TOPK)
    gate = gate.reshape(nb, PEER_TOKEN_BLOCK, PEER_HEADS * PEER_TOPK).astype(h.dtype)
    xt = hf.reshape(nb, PEER_TOKEN_BLOCK, d)

    def block(args):
        xb, ib, gb = args
        ue = jnp.take(u, ib, axis=0)
        act = jax.nn.gelu(jnp.einsum("tkd,td->tk", ue, xb), approximate=False) * gb
        return jnp.einsum("tk,tkd->td", act, jnp.take(v, ib, axis=0))

    out = lax.map(block, (xt, idx, gate))
    return out.reshape(b, s, d)


def setup_inputs(seed: int = 0) -> dict:
    key = jax.random.key(seed)
    ks = jax.random.split(key, 32)
    L, D = DEPTH, D_MODEL
    nrm = lambda k, shape: jax.random.normal(k, shape, dtype=jnp.float32)
    gain = lambda k, shape: 1.0 + 0.02 * nrm(k, shape)
    x = nrm(ks[0], (BATCH, SEQ, D))
    c = nrm(ks[1], (BATCH, D))
    offset = jax.random.randint(ks[2], (BATCH, 1), 0, 2048, dtype=jnp.int32)
    positions = (offset + jnp.arange(SEQ, dtype=jnp.int32)[None, :]).astype(jnp.int32)
    dt0 = jnp.exp(jax.random.uniform(ks[17], (L, SSD_HEADS), minval=math.log(1e-3), maxval=math.log(1e-1)))
    dt_bias = dt0 + jnp.log(-jnp.expm1(-dt0))
    a_log = jnp.log(jax.random.uniform(ks[18], (L, SSD_HEADS), minval=1.0, maxval=16.0))
    return {
        "x": x,
        "c": c,
        "positions": positions,
        "norm_mix_g": gain(ks[3], (L, D)),
        "norm_ffn_g": gain(ks[4], (L, D)),
        "w_ada": nrm(ks[5], (L, D, 6 * D)) * (0.5 * D ** -0.5),
        "b_ada": 0.02 * nrm(ks[6], (L, 6 * D)),
        "w_in": nrm(ks[7], (L, D, IN_DIM)) * D ** -0.5,
        "q_norm_g": gain(ks[8], (L, DA_QK_DIM)),
        "k_norm_g": gain(ks[9], (L, DA_QK_DIM)),
        "lam_q1": 0.1 * nrm(ks[10], (L, DA_QK_DIM)),
        "lam_k1": 0.1 * nrm(ks[11], (L, DA_QK_DIM)),
        "lam_q2": 0.1 * nrm(ks[12], (L, DA_QK_DIM)),
        "lam_k2": 0.1 * nrm(ks[13], (L, DA_QK_DIM)),
        "subln_g": gain(ks[14], (L, DA_V_DIM)),
        "conv_w": nrm(ks[15], (L, CONV_WIDTH, XBC_DIM)) * CONV_WIDTH ** -0.5,
        "conv_b": 0.02 * nrm(ks[16], (L, XBC_DIM)),
        "dt_bias": dt_bias,
        "a_log": a_log,
        "d_skip": gain(ks[19], (L, SSD_HEADS)),
        "ssd_norm_g": gain(ks[20], (L, SSD_INNER)),
        "w_out": nrm(ks[21], (L, D_MIX, D)) * D_MIX ** -0.5,
        "peer_wq": nrm(ks[22], (L, D, PEER_HEADS * PEER_QUERY_DIM)) * D ** -0.5,
        "peer_k1": nrm(ks[23], (L, PEER_HEADS, PEER_N_KEYS, PEER_HALF)) * PEER_HALF ** -0.5,
        "peer_k2": nrm(ks[24], (L, PEER_HEADS, PEER_N_KEYS, PEER_HALF)) * PEER_HALF ** -0.5,
        "peer_u": nrm(ks[25], (L, PEER_EXPERTS, D)) * D ** -0.5,
        "peer_v": nrm(ks[26], (L, PEER_EXPERTS, D)),
    }


def reference(x, c, positions, norm_mix_g, norm_ffn_g, w_ada, b_ada, w_in, q_norm_g, k_norm_g,
              lam_q1, lam_k1, lam_q2, lam_k2, subln_g, conv_w, conv_b, dt_bias, a_log, d_skip,
              ssd_norm_g, w_out, peer_wq, peer_k1, peer_k2, peer_u, peer_v):
    inv_freq = ROPE_THETA ** (-jnp.arange(0, ROT_DIM, 2, dtype=jnp.float32) / ROT_DIM)
    ang = positions.astype(jnp.float32)[..., None] * inv_freq
    cos = jnp.cos(ang)[:, :, None, None, :].astype(x.dtype)
    sin = jnp.sin(ang)[:, :, None, None, :].astype(x.dtype)
    cond = jax.nn.silu(c)
    for i in range(DEPTH):
        lam_init = 0.8 - 0.6 * math.exp(-0.3 * i)
        mod = cond @ w_ada[i] + b_ada[i]
        sh_m, sc_m, g_m, sh_f, sc_f, g_f = jnp.split(mod, 6, axis=-1)
        h = modulate(rms_norm(x, norm_mix_g[i]), sh_m, sc_m)
        x = x + g_m[:, None, :] * hybrid_mixer(
            h, cos, sin, lam_init, w_in[i], q_norm_g[i], k_norm_g[i], lam_q1[i], lam_k1[i],
            lam_q2[i], lam_k2[i], subln_g[i], conv_w[i], conv_b[i], dt_bias[i], a_log[i],
            d_skip[i], ssd_norm_g[i], w_out[i])
        h = modulate(rms_norm(x, norm_ffn_g[i]), sh_f, sc_f)
        x = x + g_f[:, None, :] * peer_ffn(h, peer_wq[i], peer_k1[i], peer_k2[i], peer_u[i], peer_v[i])
    return x
```

```python
import functools
import math

import jax
import jax.numpy as jnp
from jax import lax
from jax.experimental import pallas as pl
from jax.experimental.pallas import tpu as pltpu

F32 = jnp.float32
BF16 = jnp.bfloat16

D_MODEL = 2048
DA_WIDTH = 1024
DA_HEADS = 8
DA_V_DIM = 128
DA_QK_DIM = 64
ROT_DIM = 16
ROPE_THETA = 500000.0
SSD_INNER = 1024
SSD_HEADS = 16
SSD_HEAD_DIM = 64
SSD_GROUPS = 2
SSD_STATE = 128
SSD_CHUNK = 128
CONV_WIDTH = 4
IN_DIM = 5648
PEER_HEADS = 8
PEER_N_KEYS = 128
PEER_HALF = 128
PEER_TOPK = 16
RMS_EPS = 1e-6

LANES = 128
SUBLANES = 8
IN_DIM_PAD = 5760
VMEM_LIMIT = 52 * 1024 * 1024

_Z_BLK = 3
_XS_BLK = 4
_BC_BLK = 10
_DT_BLK = 44

NEG_BIG = -1e30


def _cparams(sem):
    return pltpu.CompilerParams(dimension_semantics=sem, vmem_limit_bytes=VMEM_LIMIT)


def _adaln_kernel(c_ref, w_ref, b_ref, o_ref):
    c = c_ref[...]
    cond = (c * jax.nn.sigmoid(c)).astype(BF16)
    w = w_ref[0].astype(BF16)
    o_ref[0] = jnp.dot(cond, w, preferred_element_type=F32) + b_ref[0]


def adaln(c, w_ada, b_ada):
    depth, d, n = w_ada.shape
    b = c.shape[0]
    tn = 1024
    return pl.pallas_call(
        _adaln_kernel,
        grid=(depth, n // tn),
        in_specs=[
            pl.BlockSpec((b, d), lambda l, j: (0, 0)),
            pl.BlockSpec((1, d, tn), lambda l, j: (l, 0, j)),
            pl.BlockSpec((1, 1, tn), lambda l, j: (l, 0, j)),
        ],
        out_specs=pl.BlockSpec((1, b, tn), lambda l, j: (l, 0, j)),
        out_shape=jax.ShapeDtypeStruct((depth, b, n), F32),
        compiler_params=_cparams(("parallel", "parallel")),
        name="adaln",
    )(c, w_ada, b_ada.reshape(depth, 1, n))


def _norm_modulate(x, g, shift, scale):
    ms = jnp.mean(x * x, axis=-1, keepdims=True)
    y = x * lax.rsqrt(ms + RMS_EPS) * g
    return y * (1.0 + scale) + shift


def _in_proj_kernel(x_ref, g_ref, sh_ref, sc_ref, w_ref, o_ref, h_scr):
    @pl.when(pl.program_id(2) == 0)
    def _():
        h = _norm_modulate(x_ref[0], g_ref[...], sh_ref[0], sc_ref[0])
        h_scr[...] = h.astype(BF16)

    o_ref[0] = jnp.dot(h_scr[...], w_ref[...], preferred_element_type=F32)


def in_proj(x, g, shift, scale, w):
    b, s, d = x.shape
    n = w.shape[1]
    tm = min(512, s)
    tn = 1152
    return pl.pallas_call(
        _in_proj_kernel,
        grid=(b, s // tm, n // tn),
        in_specs=[
            pl.BlockSpec((1, tm, d), lambda bi, i, j: (bi, i, 0)),
            pl.BlockSpec((1, d), lambda bi, i, j: (0, 0)),
            pl.BlockSpec((1, 1, d), lambda bi, i, j: (bi, 0, 0)),
            pl.BlockSpec((1, 1, d), lambda bi, i, j: (bi, 0, 0)),
            pl.BlockSpec((d, tn), lambda bi, i, j: (0, j)),
        ],
        out_specs=pl.BlockSpec((1, tm, tn), lambda bi, i, j: (bi, i, j)),
        out_shape=jax.ShapeDtypeStruct((b, s, n), F32),
        scratch_shapes=[pltpu.VMEM((tm, d), BF16)],
        compiler_params=_cparams(("parallel", "parallel", "arbitrary")),
        name="in_proj",
    )(x, g.reshape(1, d), shift.reshape(b, 1, d), scale.reshape(b, 1, d), w)


def _qk_prep_kernel(q_ref, k_ref, v_ref, cos_ref, sin_ref, gq_ref, gk_ref, seg_ref, rot_ref,
                    qo_ref, ko_ref, vo_ref):
    cos = cos_ref[0]
    sin = sin_ref[0]
    seg = seg_ref[...]
    rot_m = rot_ref[...]

    def prep(x_ref, g, o_ref, out_scale):
        for c in range(DA_WIDTH // LANES):
            sl = slice(c * LANES, (c + 1) * LANES)
            x = x_ref[0, :, sl]
            ss = jnp.dot((x * x).astype(BF16), seg, preferred_element_type=F32)
            y = x * lax.rsqrt(ss * (1.0 / DA_QK_DIM) + RMS_EPS) * g
            r = jnp.dot(y.astype(BF16), rot_m, preferred_element_type=F32)
            o_ref[0, :, sl] = ((y * cos + r * sin) * out_scale).astype(BF16)

    prep(q_ref, gq_ref[...], qo_ref, DA_QK_DIM ** -0.5)
    prep(k_ref, gk_ref[...], ko_ref, 1.0)
    vo_ref[0] = v_ref[0].astype(BF16)


def qk_prep(proj, cos_t, sin_t, gq, gk):
    b, s, _ = proj.shape
    ts = min(512, s)
    lane = jnp.arange(LANES)
    seg_m = (lane[:, None] // DA_QK_DIM == lane[None, :] // DA_QK_DIM).astype(BF16)
    off = lane % DA_QK_DIM
    half = ROT_DIM // 2
    src = lane[:, None]
    dst = lane[None, :]
    rot_m = (jnp.where((off[None, :] < half) & (src == dst + half), -1.0, 0.0)
             + jnp.where((off[None, :] >= half) & (off[None, :] < ROT_DIM) & (src == dst - half), 1.0, 0.0)
             ).astype(BF16)
    gq_t = jnp.tile(gq, LANES // DA_QK_DIM).reshape(1, LANES)
    gk_t = jnp.tile(gk, LANES // DA_QK_DIM).reshape(1, LANES)
    blk = lambda c: pl.BlockSpec((1, ts, DA_WIDTH), lambda bi, i, c=c: (bi, i, c))
    const = lambda shape: pl.BlockSpec(shape, lambda bi, i: (0,) * len(shape))
    out_sds = jax.ShapeDtypeStruct((b, s, DA_WIDTH), BF16)
    return pl.pallas_call(
        _qk_prep_kernel,
        grid=(b, s // ts),
        in_specs=[blk(0), blk(1), blk(2),
                  pl.BlockSpec((1, ts, LANES), lambda bi, i: (bi, i, 0)),
                  pl.BlockSpec((1, ts, LANES), lambda bi, i: (bi, i, 0)),
                  const((1, LANES)), const((1, LANES)),
                  const((LANES, LANES)), const((LANES, LANES))],
        out_specs=[pl.BlockSpec((1, ts, DA_WIDTH), lambda bi, i: (bi, i, 0))] * 3,
        out_shape=[out_sds] * 3,
        compiler_params=_cparams(("parallel", "parallel")),
        name="qk_prep",
    )(proj, proj, proj, cos_t, sin_t, gq_t, gk_t, seg_m, rot_m)


def _attn_kernel(lam_ref, q_ref, k_ref, v_ref, g_ref, o_ref, *, tq, tk, out_scale):
    qi = pl.program_id(2)
    q = q_ref[0]
    lane = lax.broadcasted_iota(jnp.int32, (tq, LANES), 1)
    zero = jnp.zeros_like(q)
    q0 = jnp.where(lane < DA_QK_DIM, q, zero)
    q1 = jnp.where(lane >= DA_QK_DIM, q, zero)
    row = qi * tq + lax.broadcasted_iota(jnp.int32, (tq, tk), 0)
    col0 = lax.broadcasted_iota(jnp.int32, (tq, tk), 1)
    nt = (((1,), (1,)), ((), ()))

    def update(s, m, l, acc, v):
        m_new = jnp.maximum(m, jnp.max(s, axis=-1, keepdims=True))
        p = jnp.exp(s - m_new)
        alpha = jnp.exp(m - m_new)
        l = alpha * l + jnp.sum(p, axis=-1, keepdims=True)
        acc = alpha * acc + jnp.dot(p.astype(BF16), v, preferred_element_type=F32)
        return m_new, l, acc

    def body(j, carry):
        m0, l0, a0, m1, l1, a1 = carry
        start = pl.multiple_of(j * tk, tk)
        k = k_ref[0, pl.ds(start, tk), :]
        v = v_ref[0, pl.ds(start, tk), :]
        ok = (col0 + j * tk) <= row
        s0 = jnp.where(ok, lax.dot_general(q0, k, nt, preferred_element_type=F32), NEG_BIG)
        s1 = jnp.where(ok, lax.dot_general(q1, k, nt, preferred_element_type=F32), NEG_BIG)
        m0, l0, a0 = update(s0, m0, l0, a0, v)
        m1, l1, a1 = update(s1, m1, l1, a1, v)
        return m0, l0, a0, m1, l1, a1

    mi = jnp.full((tq, 1), NEG_BIG, F32)
    li = jnp.zeros((tq, 1), F32)
    ai = jnp.zeros((tq, DA_V_DIM), F32)
    n_kv = ((qi + 1) * tq + tk - 1) // tk
    m0, l0, a0, m1, l1, a1 = lax.fori_loop(0, n_kv, body, (mi, li, ai, mi, li, ai))
    out = a0 / l0 - lam_ref[0] * (a1 / l1)
    ms = jnp.mean(out * out, axis=-1, keepdims=True)
    o_ref[0] = (out * lax.rsqrt(ms + RMS_EPS) * g_ref[...] * out_scale).astype(BF16)


def diff_attention(qn, kn, vn, lam, subln_g, lam_init):
    b, s, _ = qn.shape
    tq = tk = min(256, s)
    kern = functools.partial(_attn_kernel, tq=tq, tk=tk, out_scale=1.0 - lam_init)
    return pl.pallas_call(
        kern,
        grid=(b, DA_HEADS, s // tq),
        in_specs=[
            pl.BlockSpec(memory_space=pltpu.SMEM),
            pl.BlockSpec((1, tq, LANES), lambda bi, h, i: (bi, i, h)),
            pl.BlockSpec((1, s, LANES), lambda bi, h, i: (bi, 0, h)),
            pl.BlockSpec((1, s, LANES), lambda bi, h, i: (bi, 0, h)),
            pl.BlockSpec((1, LANES), lambda bi, h, i: (0, 0)),
        ],
        out_specs=pl.BlockSpec((1, tq, LANES), lambda bi, h, i: (bi, i, h)),
        out_shape=jax.ShapeDtypeStruct((b, s, DA_WIDTH), BF16),
        compiler_params=_cparams(("parallel", "parallel", "parallel")),
        name="diff_attn",
    )(lam.reshape(1).astype(F32), qn, kn, vn, subln_g.reshape(1, LANES))


def _split_bf16(x):
    hi = x.astype(BF16)
    lo = (x - hi.astype(F32)).astype(BF16)
    return hi, lo


def _silu(x):
    return x * jax.nn.sigmoid(x)


def _ssd_kernel(z_ref, xs_ref, bc_ref, dt_ref, cwx_ref, cwb_ref, cbx_ref, cbb_ref, dtb_ref,
                alog_ref, dsk_ref, ng_ref, tri_ref, exp_ref, o_ref, pxs, pbc, st):
    q = SSD_CHUNK

    @pl.when(pl.program_id(1) == 0)
    def _():
        pxs[...] = jnp.zeros_like(pxs)
        pbc[...] = jnp.zeros_like(pbc)
        st[...] = jnp.zeros_like(st)

    def conv(x, prev, w_ref, b_ref):
        acc = x * w_ref[CONV_WIDTH - 1:CONV_WIDTH, :] + b_ref[...]
        row8 = lax.broadcasted_iota(jnp.int32, (SUBLANES, x.shape[1]), 0)
        for k in range(1, CONV_WIDTH):
            xr = pltpu.roll(x, k, 0)
            pr = pltpu.roll(prev, k, 0)
            first = jnp.where(row8 < k, pr, xr[0:SUBLANES])
            xk = jnp.concatenate([first, xr[SUBLANES:]], axis=0)
            acc = acc + xk * w_ref[CONV_WIDTH - 1 - k:CONV_WIDTH - k, :]
        return acc

    xs_raw = xs_ref[0]
    bc_raw = bc_ref[0]
    xs = _silu(conv(xs_raw, pxs[...], cwx_ref, cbx_ref))
    bcs = _silu(conv(bc_raw, pbc[...], cwb_ref, cbb_ref))
    pxs[...] = xs_raw[q - SUBLANES:q]
    pbc[...] = bc_raw[q - SUBLANES:q]

    gw = SSD_GROUPS * SSD_STATE
    bm = [bcs[:, g * SSD_STATE:(g + 1) * SSD_STATE] for g in range(SSD_GROUPS)]
    cm = [bcs[:, gw + g * SSD_STATE:gw + (g + 1) * SSD_STATE].astype(BF16) for g in range(SSD_GROUPS)]

    v = dt_ref[0] + dtb_ref[...]
    dt = jnp.maximum(v, 0.0) + jnp.log1p(jnp.exp(-jnp.abs(v)))
    a_dt = dt * (-jnp.exp(alog_ref[...]))
    tri = tri_ref[...]
    a_hi, a_lo = _split_bf16(a_dt)
    a_cs = (jnp.dot(tri, a_hi, preferred_element_type=F32)
            + jnp.dot(tri, a_lo, preferred_element_type=F32))
    a_cs_t = a_cs.T
    exp_a = jnp.exp(a_cs)
    decay = jnp.exp(a_cs[q - 1:q, :] - a_cs)
    stack = jnp.concatenate([dt, exp_a, decay], axis=0)
    s_hi, s_lo = _split_bf16(stack)
    ex = (jnp.dot(s_hi, exp_ref[...], preferred_element_type=F32)
          + jnp.dot(s_lo, exp_ref[...], preferred_element_type=F32))
    dt_e = ex[0:q]
    exp_a_e = ex[q:2 * q]
    decay_e = ex[2 * q:3 * q]
    x_dt = xs * dt_e
    x_dec = x_dt * decay_e

    nt = (((1,), (1,)), ((), ()))
    cb = [lax.dot_general(cm[g], bm[g].astype(BF16), nt, preferred_element_type=F32)
          for g in range(SSD_GROUPS)]
    b_t = [bm[g].T.astype(BF16) for g in range(SSD_GROUPS)]
    causal = (lax.broadcasted_iota(jnp.int32, (q, q), 0) >= lax.broadcasted_iota(jnp.int32, (q, q), 1))
    lane = lax.broadcasted_iota(jnp.int32, (q, LANES), 1)
    first_head = lane < SSD_HEAD_DIM
    z = z_ref[0]
    heads_per_group = SSD_HEADS // SSD_GROUPS
    pairs = []
    for j in range(SSD_HEADS // 2):
        sl = slice(j * LANES, (j + 1) * LANES)
        g = (2 * j) // heads_per_group
        xp = x_dt[:, sl]
        y = jnp.zeros((q, LANES), F32)
        for hh in range(2):
            h = 2 * j + hh
            seg = a_cs[:, h:h + 1] - a_cs_t[h:h + 1, :]
            decay_l = jnp.exp(jnp.where(causal, seg, -jnp.inf))
            gm = (cb[g] * decay_l).astype(BF16)
            keep = first_head if hh == 0 else jnp.logical_not(first_head)
            xm = jnp.where(keep, xp, 0.0).astype(BF16)
            y = y + jnp.dot(gm, xm, preferred_element_type=F32)
        sp = st[j]
        y_off = jnp.dot(cm[g], sp.astype(BF16), preferred_element_type=F32) * exp_a_e[:, sl]
        st[j] = (sp * exp_a_e[q - 1:q, sl]
                 + jnp.dot(b_t[g], x_dec[:, sl].astype(BF16), preferred_element_type=F32))
        yp = y + y_off + dsk_ref[:, sl] * xs[:, sl]
        pairs.append(yp * _silu(z[:, sl]))
    y = jnp.concatenate(pairs, axis=1)
    gwid = SSD_INNER // SSD_GROUPS
    outs = []
    for g in range(SSD_GROUPS):
        yg = y[:, g * gwid:(g + 1) * gwid]
        ms = jnp.mean(yg * yg, axis=-1, keepdims=True)
        outs.append(yg * lax.rsqrt(ms + RMS_EPS) * ng_ref[:, g * gwid:(g + 1) * gwid])
    o_ref[0] = jnp.concatenate(outs, axis=1).astype(BF16)


def ssd_group(proj, conv_w, conv_b, dt_bias, a_log, d_skip, norm_g):
    b, s, _ = proj.shape
    q = SSD_CHUNK
    pad = lambda vec: jnp.pad(vec, (0, LANES - SSD_HEADS)).reshape(1, LANES)
    idx = jnp.arange(q)
    tri = (idx[:, None] >= idx[None, :]).astype(BF16)
    expand = (jnp.arange(LANES)[:, None] == (jnp.arange(SSD_INNER)[None, :] // SSD_HEAD_DIM)).astype(BF16)
    bcw = 2 * SSD_GROUPS * SSD_STATE
    const = lambda shape: pl.BlockSpec(shape, lambda bi, c: (0,) * len(shape))
    return pl.pallas_call(
        _ssd_kernel,
        grid=(b, s // q),
        in_specs=[
            pl.BlockSpec((1, q, SSD_INNER), lambda bi, c: (bi, c, _Z_BLK)),
            pl.BlockSpec((1, q, SSD_INNER), lambda bi, c: (bi, c, _XS_BLK)),
            pl.BlockSpec((1, q, bcw), lambda bi, c: (bi, c, _BC_BLK)),
            pl.BlockSpec((1, q, LANES), lambda bi, c: (bi, c, _DT_BLK)),
            const((CONV_WIDTH, SSD_INNER)), const((CONV_WIDTH, bcw)),
            const((1, SSD_INNER)), const((1, bcw)),
            const((1, LANES)), const((1, LANES)),
            const((1, SSD_INNER)), const((1, SSD_INNER)),
            const((q, q)), const((LANES, SSD_INNER)),
        ],
        out_specs=pl.BlockSpec((1, q, SSD_INNER), lambda bi, c: (bi, c, 0)),
        out_shape=jax.ShapeDtypeStruct((b, s, SSD_INNER), BF16),
        scratch_shapes=[pltpu.VMEM((SUBLANES, SSD_INNER), F32),
                        pltpu.VMEM((SUBLANES, bcw), F32),
                        pltpu.VMEM((SSD_HEADS // 2, SSD_STATE, LANES), F32)],
        compiler_params=_cparams(("parallel", "arbitrary")),
        name="ssd",
    )(proj, proj, proj, proj,
      conv_w[:, :SSD_INNER], conv_w[:, SSD_INNER:],
      conv_b[:SSD_INNER].reshape(1, -1), conv_b[SSD_INNER:].reshape(1, -1),
      pad(dt_bias), pad(a_log),
      jnp.repeat(d_skip, SSD_HEAD_DIM).reshape(1, -1), norm_g.reshape(1, -1),
      tri, expand)


def _out_proj_kernel(a_ref, s_ref, w_ref, x_ref, g_ref, o_ref):
    acc = jnp.dot(a_ref[0], w_ref[0:DA_WIDTH, :], preferred_element_type=F32)
    acc = acc + jnp.dot(s_ref[0], w_ref[DA_WIDTH:, :], preferred_element_type=F32)
    o_ref[0] = x_ref[0] + g_ref[0] * acc


def out_proj(attn, ssd, w, x, gate):
    b, s, d = x.shape
    tm = min(512, s)
    return pl.pallas_call(
        _out_proj_kernel,
        grid=(b, s // tm),
        in_specs=[
            pl.BlockSpec((1, tm, DA_WIDTH), lambda bi, i: (bi, i, 0)),
            pl.BlockSpec((1, tm, SSD_INNER), lambda bi, i: (bi, i, 0)),
            pl.BlockSpec((d, d), lambda bi, i: (0, 0)),
            pl.BlockSpec((1, tm, d), lambda bi, i: (bi, i, 0)),
            pl.BlockSpec((1, 1, d), lambda bi, i: (bi, 0, 0)),
        ],
        out_specs=pl.BlockSpec((1, tm, d), lambda bi, i: (bi, i, 0)),
        out_shape=jax.ShapeDtypeStruct((b, s, d), F32),
        compiler_params=_cparams(("parallel", "parallel")),
        name="out_proj",
    )(attn, ssd, w, x, gate.reshape(b, 1, d))


def _extract_topk(vals, pos, payload, k):
    n = vals.shape[1]
    rowk = lax.broadcasted_iota(jnp.int32, (k, n), 0)
    top_v = jnp.zeros((k, n), F32)
    top_p = jnp.zeros((k, n), F32)
    big = jnp.float32(1e9)
    for r in range(k):
        m = jnp.max(vals, axis=0, keepdims=True)
        p = jnp.min(jnp.where(vals == m, pos, big), axis=0, keepdims=True)
        sel = pos == p
        if payload is None:
            pay = p
        else:
            pay = jnp.max(jnp.where(sel, payload, -1.0), axis=0, keepdims=True)
        top_v = jnp.where(rowk == r, m, top_v)
        top_p = jnp.where(rowk == r, pay, top_p)
        vals = jnp.where(sel, -jnp.inf, vals)
    return top_v, top_p


def _route_kernel(x_ref, g_ref, sh_ref, sc_ref, wq_ref, k1_ref, k2_ref, h_ref, idx_ref, gate_ref):
    tm = x_ref.shape[1]
    h = _norm_modulate(x_ref[0], g_ref[...], sh_ref[0], sc_ref[0])
    h_ref[0] = h
    qv = jnp.dot(h.astype(BF16), wq_ref[...], preferred_element_type=F32)
    nt = (((1,), (1,)), ((), ()))
    key_pos = lax.broadcasted_iota(jnp.int32, (PEER_N_KEYS, tm), 0).astype(F32)
    k = PEER_TOPK
    sub = lax.broadcasted_iota(jnp.int32, (SUBLANES, tm), 0).astype(F32)
    sub16 = lax.broadcasted_iota(jnp.int32, (k, tm), 0).astype(F32)
    idx_rows = []
    gate_rows = []
    for hd in range(PEER_HEADS):
        base = hd * 2 * PEER_HALF
        qa = qv[:, base:base + PEER_HALF].astype(BF16)
        qb = qv[:, base + PEER_HALF:base + 2 * PEER_HALF].astype(BF16)
        s1 = lax.dot_general(k1_ref[hd], qa, nt, preferred_element_type=F32)
        s2 = lax.dot_general(k2_ref[hd], qb, nt, preferred_element_type=F32)
        v1, i1 = _extract_topk(s1, key_pos, None, k)
        v2, i2 = _extract_topk(s2, key_pos, None, k)
        cv = [v1[0:1] + v2]
        ci = [i1[0:1] * PEER_N_KEYS + i2]
        cp = [sub16]
        for a in range(1, SUBLANES):
            cv.append(v1[a:a + 1] + v2[0:SUBLANES])
            ci.append(i1[a:a + 1] * PEER_N_KEYS + i2[0:SUBLANES])
            cp.append(sub + float(a * k))
        cv.append(v1[SUBLANES:k] + v2[0:1])
        ci.append(i1[SUBLANES:k] * PEER_N_KEYS + i2[0:1])
        cp.append((sub + float(SUBLANES)) * float(k))
        cand_v = jnp.concatenate(cv, axis=0)
        cand_i = jnp.concatenate(ci, axis=0)
        cand_p = jnp.concatenate(cp, axis=0)
        top, eidx = _extract_topk(cand_v, cand_p, cand_i, k)
        e = jnp.exp(top - jnp.max(top, axis=0, keepdims=True))
        gate_rows.append(e / jnp.sum(e, axis=0, keepdims=True))
        idx_rows.append(eidx)
    idx_ref[0] = jnp.concatenate(idx_rows, axis=0).T.astype(jnp.int32)
    gate_ref[0] = jnp.concatenate(gate_rows, axis=0).T


def peer_route(x, g, shift, scale, wq, k1, k2):
    b, s, d = x.shape
    tm = min(256, s)
    nsel = PEER_HEADS * PEER_TOPK
    const = lambda shape: pl.BlockSpec(shape, lambda bi, i: (0,) * len(shape))
    return pl.pallas_call(
        _route_kernel,
        grid=(b, s // tm),
        in_specs=[
            pl.BlockSpec((1, tm, d), lambda bi, i: (bi, i, 0)),
            const((1, d)),
            pl.BlockSpec((1, 1, d), lambda bi, i: (bi, 0, 0)),
            pl.BlockSpec((1, 1, d), lambda bi, i: (bi, 0, 0)),
            const((d, d)),
            const((PEER_HEADS, PEER_N_KEYS, PEER_HALF)),
            const((PEER_HEADS, PEER_N_KEYS, PEER_HALF)),
        ],
        out_specs=[pl.BlockSpec((1, tm, d), lambda bi, i: (bi, i, 0)),
                   pl.BlockSpec((1, tm, nsel), lambda bi, i: (bi, i, 0)),
                   pl.BlockSpec((1, tm, nsel), lambda bi, i: (bi, i, 0))],
        out_shape=[jax.ShapeDtypeStruct((b, s, d), F32),
                   jax.ShapeDtypeStruct((b, s, nsel), jnp.int32),
                   jax.ShapeDtypeStruct((b, s, nsel), F32)],
        compiler_params=_cparams(("parallel", "parallel")),
        name="peer_route",
    )(x, g.reshape(1, d), shift.reshape(b, 1, d), scale.reshape(b, 1, d), wq, k1, k2)


_ROW_TILES = D_MODEL // LANES
_NSEL = PEER_HEADS * PEER_TOPK
_EXPERT_NBUF = 4


def _expert_kernel(idx_ref, gate_ref, h_ref, grp_ref, uv_hbm, o_ref, buf, sem, pscr, splat):
    te = h_ref.shape[0]
    nbuf = _EXPERT_NBUF

    def row_copy(t, k, slot):
        return pltpu.make_async_copy(uv_hbm.at[idx_ref[t, k]], buf.at[slot, k], sem.at[slot])

    def issue(t, slot):
        def one(k, carry):
            row_copy(t, k, slot).start()
            return carry

        lax.fori_loop(0, _NSEL, one, 0, unroll=8)

    def wait(slot):
        pltpu.make_async_copy(uv_hbm.at[pl.ds(0, _NSEL)], buf.at[slot], sem.at[slot]).wait()

    for t0 in range(nbuf - 1):
        issue(t0, t0)

    ones = jnp.ones((2 * SUBLANES, LANES), BF16)
    nt = (((1,), (1,)), ((), ()))

    def body(t, carry):
        slot = lax.rem(t, nbuf)
        nxt = t + (nbuf - 1)

        @pl.when(nxt < te)
        def _():
            issue(nxt, lax.rem(nxt, nbuf))

        wait(slot)
        h3 = h_ref[t]
        def prod_pair(kk, carry):
            parts = []
            for k in (2 * kk, 2 * kk + 1):
                pr = buf[slot, k, 0:_ROW_TILES, :] * h3
                parts.append(pr[0:SUBLANES] + pr[SUBLANES:_ROW_TILES])
            row0 = pl.multiple_of(kk * 2 * SUBLANES, 2 * SUBLANES)
            pscr[pl.ds(row0, 2 * SUBLANES), :] = jnp.concatenate(parts, axis=0).astype(BF16)
            return carry

        lax.fori_loop(0, _NSEL // 2, prod_pair, 0, unroll=8)
        r =lax.dot_general(ones, pscr[...], nt, preferred_element_type=F32)
        a = jnp.dot(r.astype(BF16), grp_ref[...], preferred_element_type=F32)
        a = a[0:1]
        act = 0.5 * a * (1.0 + lax.erf(a * (2.0 ** -0.5))) * gate_ref[pl.ds(t, 1), :]
        splat[...] = jnp.broadcast_to(act, (LANES, LANES)).T
        n_acc = 4

        def weighted_rows(kq, accs):
            accs = list(accs)
            for i in range(n_acc):
                k = kq * n_acc + i
                w = jnp.broadcast_to(splat[pl.ds(k, 1), :], (_ROW_TILES, LANES))
                accs[i] = accs[i] + w * buf[slot, k, _ROW_TILES:2 * _ROW_TILES, :]
            return tuple(accs)

        zero = jnp.zeros((_ROW_TILES, LANES), F32)
        accs = lax.fori_loop(0, _NSEL // n_acc, weighted_rows, (zero,) * n_acc, unroll=4)
        o_ref[t] = (accs[0] + accs[1]) + (accs[2] + accs[3])
        return carry

    lax.fori_loop(0, te + 0 * pl.program_id(0), body, 0)


def peer_experts(idx, gate, h3, uv):
    m = idx.shape[0]
    te = min(128, m)
    grp = (jnp.arange(_NSEL * SUBLANES)[:, None] // SUBLANES == jnp.arange(_NSEL)[None, :]).astype(BF16)
    return pl.pallas_call(
        _expert_kernel,
        grid=(m // te,),
        in_specs=[
            pl.BlockSpec((te, _NSEL), lambda i: (i, 0), memory_space=pltpu.SMEM),
            pl.BlockSpec((te, _NSEL), lambda i: (i, 0)),
            pl.BlockSpec((te, _ROW_TILES, LANES), lambda i: (i, 0, 0)),
            pl.BlockSpec((_NSEL * SUBLANES, _NSEL), lambda i: (0, 0)),
            pl.BlockSpec(memory_space=pl.ANY),
        ],
        out_specs=pl.BlockSpec((te, _ROW_TILES, LANES), lambda i: (i, 0, 0)),
        out_shape=jax.ShapeDtypeStruct((m, _ROW_TILES, LANES), F32),
        scratch_shapes=[
            pltpu.VMEM((_EXPERT_NBUF, _NSEL, 2 * _ROW_TILES, LANES), F32),
            pltpu.SemaphoreType.DMA((_EXPERT_NBUF,)),
            pltpu.VMEM((_NSEL * SUBLANES, LANES), BF16),
            pltpu.VMEM((LANES, LANES), F32),
        ],
        compiler_params=_cparams(("arbitrary",)),
        name="peer_experts",
    )(idx, gate, h3, grp, uv)


def _residual_kernel(x_ref, y_ref, g_ref, o_ref):
    o_ref[0] = x_ref[0] + g_ref[0] * y_ref[0]


def gated_residual(x, y, gate):
    b, s, d = x.shape
    tm = min(512, s)
    blk = pl.BlockSpec((1, tm, d), lambda bi, i: (bi, i, 0))
    return pl.pallas_call(
        _residual_kernel,
        grid=(b, s // tm),
        in_specs=[blk, blk, pl.BlockSpec((1, 1, d), lambda bi, i: (bi, 0, 0))],
        out_specs=blk,
        out_shape=jax.ShapeDtypeStruct((b, s, d), F32),
        compiler_params=_cparams(("parallel", "parallel")),
        name="gated_residual",
    )(x, y, gate.reshape(b, 1, d))


def _rope_tables(positions):
    inv_freq = ROPE_THETA ** (-jnp.arange(0, ROT_DIM, 2, dtype=F32) / ROT_DIM)
    ang = positions.astype(F32)[..., None] * inv_freq
    cos = jnp.cos(ang)
    sin = jnp.sin(ang)
    b, s, _ = ang.shape
    rest = DA_QK_DIM - ROT_DIM
    cos64 = jnp.concatenate([cos, cos, jnp.ones((b, s, rest), F32)], axis=-1)
    sin64 = jnp.concatenate([sin, sin, jnp.zeros((b, s, rest), F32)], axis=-1)
    return jnp.tile(cos64, (1, 1, LANES // DA_QK_DIM)), jnp.tile(sin64, (1, 1, LANES // DA_QK_DIM))


def kernel(x, c, positions, norm_mix_g, norm_ffn_g, w_ada, b_ada, w_in, q_norm_g, k_norm_g, lam_q1, lam_k1, lam_q2, lam_k2, subln_g, conv_w, conv_b, dt_bias, a_log, d_skip, ssd_norm_g, w_out, peer_wq, peer_k1, peer_k2, peer_u, peer_v):
    depth = w_ada.shape[0]
    b, s, d = x.shape
    cos_t, sin_t = _rope_tables(positions)
    mod = adaln(c, w_ada, b_ada)
    for i in range(depth):
        lam_init = 0.8 - 0.6 * math.exp(-0.3 * i)
        sh_m, sc_m, g_m, sh_f, sc_f, g_f = [mod[i, :, j * d:(j + 1) * d] for j in range(6)]
        w_in_p = jnp.pad(w_in[i], ((0, 0), (0, IN_DIM_PAD - IN_DIM))).astype(BF16)
        proj = in_proj(x, norm_mix_g[i], sh_m, sc_m, w_in_p)
        qn, kn, vn = qk_prep(proj, cos_t, sin_t, q_norm_g[i], k_norm_g[i])
        lam = (jnp.exp(jnp.sum(lam_q1[i] * lam_k1[i])) - jnp.exp(jnp.sum(lam_q2[i] * lam_k2[i])) + lam_init)
        attn = diff_attention(qn, kn, vn, lam, subln_g[i], lam_init)
        ssd = ssd_group(proj, conv_w[i], conv_b[i], dt_bias[i], a_log[i], d_skip[i], ssd_norm_g[i])
        x = out_proj(attn, ssd, w_out[i].astype(BF16), x, g_m)
        h2, idx, gate = peer_route(x, norm_ffn_g[i], sh_f, sc_f, peer_wq[i].astype(BF16),
                                   peer_k1[i].astype(BF16), peer_k2[i].astype(BF16))
        n_exp = peer_u.shape[1]
        uv = jnp.concatenate([peer_u[i].reshape(n_exp, _ROW_TILES, LANES),
                              peer_v[i].reshape(n_exp, _ROW_TILES, LANES)], axis=1)
        y3 = peer_experts(idx.reshape(b * s, _NSEL), gate.reshape(b * s, _NSEL),
                          h2.reshape(b * s, _ROW_TILES, LANES), uv)
        x = gated_residual(x, y3.reshape(b, s, d), g_f)
    return x
```

```python
import functools
import math

import jax
import jax.numpy as jnp
from jax import lax
from jax.experimental import pallas as pl
from jax.experimental.pallas import tpu as pltpu

F32 = jnp.float32
BF16 = jnp.bfloat16

D_MODEL = 2048
DA_WIDTH = 1024
DA_HEADS = 8
DA_V_DIM = 128
DA_QK_DIM = 64
ROT_DIM = 16
ROPE_THETA = 500000.0
SSD_INNER = 1024
SSD_HEADS = 16
SSD_HEAD_DIM = 64
SSD_GROUPS = 2
SSD_STATE = 128
SSD_CHUNK = 128
CONV_WIDTH = 4
IN_DIM = 5648
PEER_HEADS = 8
PEER_N_KEYS = 128
PEER_HALF = 128
PEER_TOPK = 16
RMS_EPS = 1e-6

LANES = 128
SUBLANES = 8
IN_DIM_PAD = 5760
VMEM_LIMIT = 52 * 1024 * 1024

_Z_BLK = 3
_XS_BLK = 4
_BC_BLK = 10
_DT_BLK = 44

NEG_BIG = -1e30


def _cparams(sem):
    return pltpu.CompilerParams(dimension_semantics=sem, vmem_limit_bytes=VMEM_LIMIT)


def _adaln_kernel(c_ref, w_ref, b_ref, o_ref):
    c = c_ref[...]
    cond = (c * jax.nn.sigmoid(c)).astype(BF16)
    w = w_ref[0].astype(BF16)
    o_ref[0] = jnp.dot(cond, w, preferred_element_type=F32) + b_ref[0]


def adaln(c, w_ada, b_ada):
    depth, d, n = w_ada.shape
    b = c.shape[0]
    tn = 1024
    return pl.pallas_call(
        _adaln_kernel,
        grid=(depth, n // tn),
        in_specs=[
            pl.BlockSpec((b, d), lambda l, j: (0, 0)),
            pl.BlockSpec((1, d, tn), lambda l, j: (l, 0, j)),
            pl.BlockSpec((1, 1, tn), lambda l, j: (l, 0, j)),
        ],
        out_specs=pl.BlockSpec((1, b, tn), lambda l, j: (l, 0, j)),
        out_shape=jax.ShapeDtypeStruct((depth, b, n), F32),
        compiler_params=_cparams(("parallel", "parallel")),
        name="adaln",
    )(c, w_ada, b_ada.reshape(depth, 1, n))


def _norm_modulate(x, g, shift, scale):
    ms = jnp.mean(x * x, axis=-1, keepdims=True)
    y = x * lax.rsqrt(ms + RMS_EPS) * g
    return y * (1.0 + scale) + shift


def _in_proj_kernel(x_ref, g_ref, sh_ref, sc_ref, w_ref, o_ref, h_scr):
    @pl.when(pl.program_id(2) == 0)
    def _():
        h = _norm_modulate(x_ref[0], g_ref[...], sh_ref[0], sc_ref[0])
        h_scr[...] = h.astype(BF16)

    o_ref[0] = jnp.dot(h_scr[...], w_ref[...], preferred_element_type=F32)


def in_proj(x, g, shift, scale, w):
    b, s, d = x.shape
    n = w.shape[1]
    tm = min(512, s)
    tn = 1152
    return pl.pallas_call(
        _in_proj_kernel,
        grid=(b, s // tm, n // tn),
        in_specs=[
            pl.BlockSpec((1, tm, d), lambda bi, i, j: (bi, i, 0)),
            pl.BlockSpec((1, d), lambda bi, i, j: (0, 0)),
            pl.BlockSpec((1, 1, d), lambda bi, i, j: (bi, 0, 0)),
            pl.BlockSpec((1, 1, d), lambda bi, i, j: (bi, 0, 0)),
            pl.BlockSpec((d, tn), lambda bi, i, j: (0, j)),
        ],
        out_specs=pl.BlockSpec((1, tm, tn), lambda bi, i, j: (bi, i, j)),
        out_shape=jax.ShapeDtypeStruct((b, s, n), F32),
        scratch_shapes=[pltpu.VMEM((tm, d), BF16)],
        compiler_params=_cparams(("parallel", "parallel", "arbitrary")),
        name="in_proj",
    )(x, g.reshape(1, d), shift.reshape(b, 1, d), scale.reshape(b, 1, d), w)


def _qk_prep_kernel(q_ref, k_ref, v_ref, cos_ref, sin_ref, gq_ref, gk_ref, seg_ref, rot_ref,
                    qo_ref, ko_ref, vo_ref):
    cos = cos_ref[0]
    sin = sin_ref[0]
    seg = seg_ref[...]
    rot_m = rot_ref[...]

    def prep(x_ref, g, o_ref, out_scale):
        for c in range(DA_WIDTH // LANES):
            sl = slice(c * LANES, (c + 1) * LANES)
            x = x_ref[0, :, sl]
            ss = jnp.dot((x * x).astype(BF16), seg, preferred_element_type=F32)
            y = x * lax.rsqrt(ss * (1.0 / DA_QK_DIM) + RMS_EPS) * g
            r = jnp.dot(y.astype(BF16), rot_m, preferred_element_type=F32)
            o_ref[0, :, sl] = ((y * cos + r * sin) * out_scale).astype(BF16)

    prep(q_ref, gq_ref[...], qo_ref, DA_QK_DIM ** -0.5)
    prep(k_ref, gk_ref[...], ko_ref, 1.0)
    vo_ref[0] = v_ref[0].astype(BF16)


def qk_prep(proj, cos_t, sin_t, gq, gk):
    b, s, _ = proj.shape
    ts = min(512, s)
    lane = jnp.arange(LANES)
    seg_m = (lane[:, None] // DA_QK_DIM == lane[None, :] // DA_QK_DIM).astype(BF16)
    off = lane % DA_QK_DIM
    half = ROT_DIM // 2
    src = lane[:, None]
    dst = lane[None, :]
    rot_m = (jnp.where((off[None, :] < half) & (src == dst + half), -1.0, 0.0)
             + jnp.where((off[None, :] >= half) & (off[None, :] < ROT_DIM) & (src == dst - half), 1.0, 0.0)
             ).astype(BF16)
    gq_t = jnp.tile(gq, LANES // DA_QK_DIM).reshape(1, LANES)
    gk_t = jnp.tile(gk, LANES // DA_QK_DIM).reshape(1, LANES)
    blk = lambda c: pl.BlockSpec((1, ts, DA_WIDTH), lambda bi, i, c=c: (bi, i, c))
    const = lambda shape: pl.BlockSpec(shape, lambda bi, i: (0,) * len(shape))
    out_sds = jax.ShapeDtypeStruct((b, s, DA_WIDTH), BF16)
    return pl.pallas_call(
        _qk_prep_kernel,
        grid=(b, s // ts),
        in_specs=[blk(0), blk(1), blk(2),
                  pl.BlockSpec((1, ts, LANES), lambda bi, i: (bi, i, 0)),
                  pl.BlockSpec((1, ts, LANES), lambda bi, i: (bi, i, 0)),
                  const((1, LANES)), const((1, LANES)),
                  const((LANES, LANES)), const((LANES, LANES))],
        out_specs=[pl.BlockSpec((1, ts, DA_WIDTH), lambda bi, i: (bi, i, 0))] * 3,
        out_shape=[out_sds] * 3,
        compiler_params=_cparams(("parallel", "parallel")),
        name="qk_prep",
    )(proj, proj, proj, cos_t, sin_t, gq_t, gk_t, seg_m, rot_m)


def _attn_kernel(lam_ref, q_ref, k_ref, v_ref, g_ref, o_ref, *, tq, tk, out_scale):
    qi = pl.program_id(2)
    q = q_ref[0]
    lane = lax.broadcasted_iota(jnp.int32, (tq, LANES), 1)
    zero = jnp.zeros_like(q)
    q0 = jnp.where(lane < DA_QK_DIM, q, zero)
    q1 = jnp.where(lane >= DA_QK_DIM, q, zero)
    row = qi * tq + lax.broadcasted_iota(jnp.int32, (tq, tk), 0)
    col0 = lax.broadcasted_iota(jnp.int32, (tq, tk), 1)
    nt = (((1,), (1,)), ((), ()))

    def update(s, m, l, acc, v):
        m_new = jnp.maximum(m, jnp.max(s, axis=-1, keepdims=True))
        p = jnp.exp(s - m_new)
        alpha = jnp.exp(m - m_new)
        l = alpha * l + jnp.sum(p, axis=-1, keepdims=True)
        acc = alpha * acc + jnp.dot(p.astype(BF16), v, preferred_element_type=F32)
        return m_new, l, acc

    def body(j, carry):
        m0, l0, a0, m1, l1, a1 = carry
        start = pl.multiple_of(j * tk, tk)
        k = k_ref[0, pl.ds(start, tk), :]
        v = v_ref[0, pl.ds(start, tk), :]
        ok = (col0 + j * tk) <= row
        s0 = jnp.where(ok, lax.dot_general(q0, k, nt, preferred_element_type=F32), NEG_BIG)
        s1 = jnp.where(ok, lax.dot_general(q1, k, nt, preferred_element_type=F32), NEG_BIG)
        m0, l0, a0 = update(s0, m0, l0, a0, v)
        m1, l1, a1 = update(s1, m1, l1, a1, v)
        return m0, l0, a0, m1, l1, a1

    mi = jnp.full((tq, 1), NEG_BIG, F32)
    li = jnp.zeros((tq, 1), F32)
    ai = jnp.zeros((tq, DA_V_DIM), F32)
    n_kv = ((qi + 1) * tq + tk - 1) // tk
    m0, l0, a0, m1, l1, a1 = lax.fori_loop(0, n_kv, body, (mi, li, ai, mi, li, ai))
    out = a0 / l0 - lam_ref[0] * (a1 / l1)
    ms = jnp.mean(out * out, axis=-1, keepdims=True)
    o_ref[0] = (out * lax.rsqrt(ms + RMS_EPS) * g_ref[...] * out_scale).astype(BF16)


def diff_attention(qn, kn, vn, lam, subln_g, lam_init):
    b, s, _ = qn.shape
    tq = tk = min(256, s)
    kern = functools.partial(_attn_kernel, tq=tq, tk=tk, out_scale=1.0 - lam_init)
    return pl.pallas_call(
        kern,
        grid=(b, DA_HEADS, s // tq),
        in_specs=[
            pl.BlockSpec(memory_space=pltpu.SMEM),
            pl.BlockSpec((1, tq, LANES), lambda bi, h, i: (bi, i, h)),
            pl.BlockSpec((1, s, LANES), lambda bi, h, i: (bi, 0, h)),
            pl.BlockSpec((1, s, LANES), lambda bi, h, i: (bi, 0, h)),
            pl.BlockSpec((1, LANES), lambda bi, h, i: (0, 0)),
        ],
        out_specs=pl.BlockSpec((1, tq, LANES), lambda bi, h, i: (bi, i, h)),
        out_shape=jax.ShapeDtypeStruct((b, s, DA_WIDTH), BF16),
        compiler_params=_cparams(("parallel", "parallel", "parallel")),
        name="diff_attn",
    )(lam.reshape(1).astype(F32), qn, kn, vn, subln_g.reshape(1, LANES))


def _split_bf16(x):
    hi = x.astype(BF16)
    lo = (x - hi.astype(F32)).astype(BF16)
    return hi, lo


def _silu(x):
    return x * jax.nn.sigmoid(x)


def _ssd_kernel(z_ref, xs_ref, bc_ref, dt_ref, cwx_ref, cwb_ref, cbx_ref, cbb_ref, dtb_ref,
                alog_ref, dsk_ref, ng_ref, tri_ref, exp_ref, o_ref, pxs, pbc, st):
    q = SSD_CHUNK

    @pl.when(pl.program_id(1) == 0)
    def _():
        pxs[...] = jnp.zeros_like(pxs)
        pbc[...] = jnp.zeros_like(pbc)
        st[...] = jnp.zeros_like(st)

    def conv(x, prev, w_ref, b_ref):
        acc = x * w_ref[CONV_WIDTH - 1:CONV_WIDTH, :] + b_ref[...]
        row8 = lax.broadcasted_iota(jnp.int32, (SUBLANES, x.shape[1]), 0)
        for k in range(1, CONV_WIDTH):
            xr = pltpu.roll(x, k, 0)
            pr = pltpu.roll(prev, k, 0)
            first = jnp.where(row8 < k, pr, xr[0:SUBLANES])
            xk = jnp.concatenate([first, xr[SUBLANES:]], axis=0)
            acc = acc + xk * w_ref[CONV_WIDTH - 1 - k:CONV_WIDTH - k, :]
        return acc

    xs_raw = xs_ref[0]
    bc_raw = bc_ref[0]
    xs = _silu(conv(xs_raw, pxs[...], cwx_ref, cbx_ref))
    bcs = _silu(conv(bc_raw, pbc[...], cwb_ref, cbb_ref))
    pxs[...] = xs_raw[q - SUBLANES:q]
    pbc[...] = bc_raw[q - SUBLANES:q]

    gw = SSD_GROUPS * SSD_STATE
    bm = [bcs[:, g * SSD_STATE:(g + 1) * SSD_STATE] for g in range(SSD_GROUPS)]
    cm = [bcs[:, gw + g * SSD_STATE:gw + (g + 1) * SSD_STATE].astype(BF16) for g in range(SSD_GROUPS)]

    v = dt_ref[0] + dtb_ref[...]
    dt = jnp.maximum(v, 0.0) + jnp.log1p(jnp.exp(-jnp.abs(v)))
    a_dt = dt * (-jnp.exp(alog_ref[...]))
    tri = tri_ref[...]
    a_hi, a_lo = _split_bf16(a_dt)
    a_cs = (jnp.dot(tri, a_hi, preferred_element_type=F32)
            + jnp.dot(tri, a_lo, preferred_element_type=F32))
    a_cs_t = a_cs.T
    exp_a = jnp.exp(a_cs)
    decay = jnp.exp(a_cs[q - 1:q, :] - a_cs)
    stack = jnp.concatenate([dt, exp_a, decay], axis=0)
    s_hi, s_lo = _split_bf16(stack)
    ex = (jnp.dot(s_hi, exp_ref[...], preferred_element_type=F32)
          + jnp.dot(s_lo, exp_ref[...], preferred_element_type=F32))
    dt_e = ex[0:q]
    exp_a_e = ex[q:2 * q]
    decay_e = ex[2 * q:3 * q]
    x_dt = xs * dt_e
    x_dec = x_dt * decay_e

    nt = (((1,), (1,)), ((), ()))
    cb = [lax.dot_general(cm[g], bm[g].astype(BF16), nt, preferred_element_type=F32)
          for g in range(SSD_GROUPS)]
    b_t = [bm[g].T.astype(BF16) for g in range(SSD_GROUPS)]
    causal = (lax.broadcasted_iota(jnp.int32, (q, q), 0) >= lax.broadcasted_iota(jnp.int32, (q, q), 1))
    lane = lax.broadcasted_iota(jnp.int32, (q, LANES), 1)
    first_head = lane < SSD_HEAD_DIM
    z = z_ref[0]
    heads_per_group = SSD_HEADS // SSD_GROUPS
    pairs = []
    for j in range(SSD_HEADS // 2):
        sl = slice(j * LANES, (j + 1) * LANES)
        g = (2 * j) // heads_per_group
        xp = x_dt[:, sl]
        y = jnp.zeros((q, LANES), F32)
        for hh in range(2):
            h = 2 * j + hh
            seg = a_cs[:, h:h + 1] - a_cs_t[h:h + 1, :]
            decay_l = jnp.exp(jnp.where(causal, seg, -jnp.inf))
            gm = (cb[g] * decay_l).astype(BF16)
            keep = first_head if hh == 0 else jnp.logical_not(first_head)
            xm = jnp.where(keep, xp, 0.0).astype(BF16)
            y = y + jnp.dot(gm, xm, preferred_element_type=F32)
        sp = st[j]
        y_off = jnp.dot(cm[g], sp.astype(BF16), preferred_element_type=F32) * exp_a_e[:, sl]
        st[j] = (sp * exp_a_e[q - 1:q, sl]
                 + jnp.dot(b_t[g], x_dec[:, sl].astype(BF16), preferred_element_type=F32))
        yp = y + y_off + dsk_ref[:, sl] * xs[:, sl]
        pairs.append(yp * _silu(z[:, sl]))
    y = jnp.concatenate(pairs, axis=1)
    gwid = SSD_INNER // SSD_GROUPS
    outs = []
    for g in range(SSD_GROUPS):
        yg = y[:, g * gwid:(g + 1) * gwid]
        ms = jnp.mean(yg * yg, axis=-1, keepdims=True)
        outs.append(yg * lax.rsqrt(ms + RMS_EPS) * ng_ref[:, g * gwid:(g + 1) * gwid])
    o_ref[0] = jnp.concatenate(outs, axis=1).astype(BF16)


def ssd_group(proj, conv_w, conv_b, dt_bias, a_log, d_skip, norm_g):
    b, s, _ = proj.shape
    q = SSD_CHUNK
    pad = lambda vec: jnp.pad(vec, (0, LANES - SSD_HEADS)).reshape(1, LANES)
    idx = jnp.arange(q)
    tri = (idx[:, None] >= idx[None, :]).astype(BF16)
    expand = (jnp.arange(LANES)[:, None] == (jnp.arange(SSD_INNER)[None, :] // SSD_HEAD_DIM)).astype(BF16)
    bcw = 2 * SSD_GROUPS * SSD_STATE
    const = lambda shape: pl.BlockSpec(shape, lambda bi, c: (0,) * len(shape))
    return pl.pallas_call(
        _ssd_kernel,
        grid=(b, s // q),
        in_specs=[
            pl.BlockSpec((1, q, SSD_INNER), lambda bi, c: (bi, c, _Z_BLK)),
            pl.BlockSpec((1, q, SSD_INNER), lambda bi, c: (bi, c, _XS_BLK)),
            pl.BlockSpec((1, q, bcw), lambda bi, c: (bi, c, _BC_BLK)),
            pl.BlockSpec((1, q, LANES), lambda bi, c: (bi, c, _DT_BLK)),
            const((CONV_WIDTH, SSD_INNER)), const((CONV_WIDTH, bcw)),
            const((1, SSD_INNER)), const((1, bcw)),
            const((1, LANES)), const((1, LANES)),
            const((1, SSD_INNER)), const((1, SSD_INNER)),
            const((q, q)), const((LANES, SSD_INNER)),
        ],
        out_specs=pl.BlockSpec((1, q, SSD_INNER), lambda bi, c: (bi, c, 0)),
        out_shape=jax.ShapeDtypeStruct((b, s, SSD_INNER), BF16),
        scratch_shapes=[pltpu.VMEM((SUBLANES, SSD_INNER), F32),
                        pltpu.VMEM((SUBLANES, bcw), F32),
                        pltpu.VMEM((SSD_HEADS // 2, SSD_STATE, LANES), F32)],
        compiler_params=_cparams(("parallel", "arbitrary")),
        name="ssd",
    )(proj, proj, proj, proj,
      conv_w[:, :SSD_INNER], conv_w[:, SSD_INNER:],
      conv_b[:SSD_INNER].reshape(1, -1), conv_b[SSD_INNER:].reshape(1, -1),
      pad(dt_bias), pad(a_log),
      jnp.repeat(d_skip, SSD_HEAD_DIM).reshape(1, -1), norm_g.reshape(1, -1),
      tri, expand)


def _out_proj_kernel(a_ref, s_ref, w_ref, x_ref, g_ref, o_ref):
    acc = jnp.dot(a_ref[0], w_ref[0:DA_WIDTH, :], preferred_element_type=F32)
    acc = acc + jnp.dot(s_ref[0], w_ref[DA_WIDTH:, :], preferred_element_type=F32)
    o_ref[0] = x_ref[0] + g_ref[0] * acc


def out_proj(attn, ssd, w, x, gate):
    b, s, d = x.shape
    tm = min(512, s)
    return pl.pallas_call(
        _out_proj_kernel,
        grid=(b, s // tm),
        in_specs=[
            pl.BlockSpec((1, tm, DA_WIDTH), lambda bi, i: (bi, i, 0)),
            pl.BlockSpec((1, tm, SSD_INNER), lambda bi, i: (bi, i, 0)),
            pl.BlockSpec((d, d), lambda bi, i: (0, 0)),
            pl.BlockSpec((1, tm, d), lambda bi, i: (bi, i, 0)),
            pl.BlockSpec((1, 1, d), lambda bi, i: (bi, 0, 0)),
        ],
        out_specs=pl.BlockSpec((1, tm, d), lambda bi, i: (bi, i, 0)),
        out_shape=jax.ShapeDtypeStruct((b, s, d), F32),
        compiler_params=_cparams(("parallel", "parallel")),
        name="out_proj",
    )(attn, ssd, w, x, gate.reshape(b, 1, d))


def _extract_topk(vals, pos, payload, k):
    n = vals.shape[1]
    rowk = lax.broadcasted_iota(jnp.int32, (k, n), 0)
    top_v = jnp.zeros((k, n), F32)
    top_p = jnp.zeros((k, n), F32)
    big = jnp.float32(1e9)
    for r in range(k):
        m = jnp.max(vals, axis=0, keepdims=True)
        p = jnp.min(jnp.where(vals == m, pos, big), axis=0, keepdims=True)
        sel = pos == p
        if payload is None:
            pay = p
        else:
            pay = jnp.max(jnp.where(sel, payload, -1.0), axis=0, keepdims=True)
        top_v = jnp.where(rowk == r, m, top_v)
        top_p = jnp.where(rowk == r, pay, top_p)
        vals = jnp.where(sel, -jnp.inf, vals)
    return top_v, top_p


def _route_kernel(x_ref, g_ref, sh_ref, sc_ref, wq_ref, k1_ref, k2_ref, h_ref, idx_ref, gate_ref):
    tm = x_ref.shape[1]
    h = _norm_modulate(x_ref[0], g_ref[...], sh_ref[0], sc_ref[0])
    h_ref[0] = h
    qv = jnp.dot(h.astype(BF16), wq_ref[...], preferred_element_type=F32)
    nt = (((1,), (1,)), ((), ()))
    key_pos = lax.broadcasted_iota(jnp.int32, (PEER_N_KEYS, tm), 0).astype(F32)
    k = PEER_TOPK
    sub = lax.broadcasted_iota(jnp.int32, (SUBLANES, tm), 0).astype(F32)
    sub16 = lax.broadcasted_iota(jnp.int32, (k, tm), 0).astype(F32)
    idx_rows = []
    gate_rows = []
    for hd in range(PEER_HEADS):
        base = hd * 2 * PEER_HALF
        qa = qv[:, base:base + PEER_HALF].astype(BF16)
        qb = qv[:, base + PEER_HALF:base + 2 * PEER_HALF].astype(BF16)
        s1 = lax.dot_general(k1_ref[hd], qa, nt, preferred_element_type=F32)
        s2 = lax.dot_general(k2_ref[hd], qb, nt, preferred_element_type=F32)
        v1, i1 = _extract_topk(s1, key_pos, None, k)
        v2, i2 = _extract_topk(s2, key_pos, None, k)
        cv = [v1[0:1] + v2]
        ci = [i1[0:1] * PEER_N_KEYS + i2]
        cp = [sub16]
        for a in range(1, SUBLANES):
            cv.append(v1[a:a + 1] + v2[0:SUBLANES])
            ci.append(i1[a:a + 1] * PEER_N_KEYS + i2[0:SUBLANES])
            cp.append(sub + float(a * k))
        cv.append(v1[SUBLANES:k] + v2[0:1])
        ci.append(i1[SUBLANES:k] * PEER_N_KEYS + i2[0:1])
        cp.append((sub + float(SUBLANES)) * float(k))
        cand_v = jnp.concatenate(cv, axis=0)
        cand_i = jnp.concatenate(ci, axis=0)
        cand_p = jnp.concatenate(cp, axis=0)
        top, eidx = _extract_topk(cand_v, cand_p, cand_i, k)
        e = jnp.exp(top - jnp.max(top, axis=0, keepdims=True))
        gate_rows.append(e / jnp.sum(e, axis=0, keepdims=True))
        idx_rows.append(eidx)
    idx_ref[0] = jnp.concatenate(idx_rows, axis=0).T.astype(jnp.int32)
    gate_ref[0] = jnp.concatenate(gate_rows, axis=0).T


def peer_route(x, g, shift, scale, wq, k1, k2):
    b, s, d = x.shape
    tm = min(256, s)
    nsel = PEER_HEADS * PEER_TOPK
    const = lambda shape: pl.BlockSpec(shape, lambda bi, i: (0,) * len(shape))
    return pl.pallas_call(
        _route_kernel,
        grid=(b, s // tm),
        in_specs=[
            pl.BlockSpec((1, tm, d), lambda bi, i: (bi, i, 0)),
            const((1, d)),
            pl.BlockSpec((1, 1, d), lambda bi, i: (bi, 0, 0)),
            pl.BlockSpec((1, 1, d), lambda bi, i: (bi, 0, 0)),
            const((d, d)),
            const((PEER_HEADS, PEER_N_KEYS, PEER_HALF)),
            const((PEER_HEADS, PEER_N_KEYS, PEER_HALF)),
        ],
        out_specs=[pl.BlockSpec((1, tm, d), lambda bi, i: (bi, i, 0)),
                   pl.BlockSpec((1, tm, nsel), lambda bi, i: (bi, i, 0)),
                   pl.BlockSpec((1, tm, nsel), lambda bi, i: (bi, i, 0))],
        out_shape=[jax.ShapeDtypeStruct((b, s, d), F32),
                   jax.ShapeDtypeStruct((b, s, nsel), jnp.int32),
                   jax.ShapeDtypeStruct((b, s, nsel), F32)],
        compiler_params=_cparams(("parallel", "parallel")),
        name="peer_route",
    )(x, g.reshape(1, d), shift.reshape(b, 1, d), scale.reshape(b, 1, d), wq, k1, k2)


_ROW_TILES = D_MODEL // LANES
_NSEL = PEER_HEADS * PEER_TOPK
_GROUP = 4
_GROUPS_IN_FLIGHT = 3
_PACK = 2 * SUBLANES


def _expert_kernel(idx_ref, gate_ref, h_ref, sel_ref, uv_hbm, o_ref, buf, sem, pscr, splat):
    te = h_ref.shape[0]
    n_groups = te // _GROUP
    ones = jnp.ones((_PACK, LANES), BF16)
    nt = (((1,), (1,)), ((), ()))

    def row_copies(g):
        slot = lax.rem(g, _GROUPS_IN_FLIGHT)
        starts = []
        for i in range(_GROUP):
            t = g * _GROUP + i
            for k in range(_NSEL):
                starts.append(functools.partial(
                    lambda t, i, k: pltpu.make_async_copy(
                        uv_hbm.at[idx_ref[t, k]], buf.at[slot, i, k], sem.at[slot]).start(), t, i, k))
        return starts

    def issue(g):
        for start in row_copies(g):
            start()

    def wait(g):
        slot = lax.rem(g, _GROUPS_IN_FLIGHT)
        for i in range(_GROUP):
            pltpu.make_async_copy(uv_hbm.at[pl.ds(0, _NSEL)], buf.at[slot, i], sem.at[slot]).wait()

    def start_next(pending, n):
        for _ in range(n):
            start = next(pending, None)
            if start is not None:
                start()

    def u_side(g, pending=iter(())):
        slot = lax.rem(g, _GROUPS_IN_FLIGHT)
        par = lax.rem(g, 2)
        for i in range(_GROUP):
            t = g * _GROUP + i
            h3 = h_ref[t]
            for kk in range(_NSEL // 2):
                parts = []
                for k in (2 * kk, 2 * kk + 1):
                    pr = buf[slot, i, k, 0:_ROW_TILES, :].astype(F32) * h3
                    parts.append(pr[0:SUBLANES] + pr[SUBLANES:_ROW_TILES])
                pscr[i, kk * _PACK:(kk + 1) * _PACK, :] = jnp.concatenate(parts, axis=0).astype(BF16)
                start_next(pending, 1)
            m = jnp.dot(sel_ref[...], pscr[i], preferred_element_type=F32)
            m_hi, m_lo = _split_bf16(m)
            a = (lax.dot_general(ones, m_hi, nt, preferred_element_type=F32)
                 + lax.dot_general(ones, m_lo, nt, preferred_element_type=F32))[0:1]
            act = 0.5 * a * (1.0 + lax.erf(a * (2.0 ** -0.5))) * gate_ref[pl.ds(t, 1), :]
            splat[par, i] = jnp.broadcast_to(act, (LANES, LANES)).T

    def v_side(g, pending=iter(())):
        slot = lax.rem(g, _GROUPS_IN_FLIGHT)
        par = lax.rem(g, 2)
        n_acc = 4
        for i in range(_GROUP):
            t = g * _GROUP + i
            accs = [jnp.zeros((_ROW_TILES, LANES), F32) for _ in range(n_acc)]
            for k in range(_NSEL):
                w = jnp.broadcast_to(splat[par, i, pl.ds(k, 1), :], (_ROW_TILES, LANES))
                accs[k % n_acc] = accs[k % n_acc] + w * buf[slot, i, k, _ROW_TILES:2 * _ROW_TILES, :].astype(F32)
                start_next(pending, k % 2)
            o_ref[t] = (accs[0] + accs[1]) + (accs[2] + accs[3])

    issue(0)
    issue(1)
    wait(0)
    u_side(0)

    def steady(g, carry):
        pending = iter(row_copies(g + 2))
        wait(g + 1)
        u_side(g + 1, pending)
        v_side(g, pending)
        start_next(pending, _GROUP * _NSEL)
        return carry

    zero = 0 * pl.program_id(0)
    lax.fori_loop(zero, zero + (n_groups - 2), steady, 0)
    last = zero + (n_groups - 2)
    wait(last + 1)
    u_side(last + 1)
    v_side(last)
    v_side(last + 1)


def peer_experts(idx, gate, h3, uv):
    m = idx.shape[0]
    te = min(128, m)
    sel = (jnp.arange(_NSEL)[:, None] == jnp.arange(_NSEL * SUBLANES)[None, :] // SUBLANES).astype(BF16)
    return pl.pallas_call(
        _expert_kernel,
        grid=(m // te,),
        in_specs=[
            pl.BlockSpec((te, _NSEL), lambda i: (i, 0), memory_space=pltpu.SMEM),
            pl.BlockSpec((te, _NSEL), lambda i: (i, 0)),
            pl.BlockSpec((te, _ROW_TILES, LANES), lambda i: (i, 0, 0)),
            pl.BlockSpec((_NSEL, _NSEL * SUBLANES), lambda i: (0, 0)),
            pl.BlockSpec(memory_space=pl.ANY),
        ],
        out_specs=pl.BlockSpec((te, _ROW_TILES, LANES), lambda i: (i, 0, 0)),
        out_shape=jax.ShapeDtypeStruct((m, _ROW_TILES, LANES), F32),
        scratch_shapes=[
            pltpu.VMEM((_GROUPS_IN_FLIGHT, _GROUP, _NSEL, 2 * _ROW_TILES, LANES), BF16),
            pltpu.SemaphoreType.DMA((_GROUPS_IN_FLIGHT,)),
            pltpu.VMEM((_GROUP, _NSEL * SUBLANES, LANES), BF16),
            pltpu.VMEM((2, _GROUP, LANES, LANES), F32),
        ],
        compiler_params=_cparams(("arbitrary",)),
        name="peer_experts",
    )(idx, gate, h3, sel, uv)


def _residual_kernel(x_ref, y_ref, g_ref, o_ref):
    o_ref[0] = x_ref[0] + g_ref[0] * y_ref[0]


def gated_residual(x, y, gate):
    b, s, d = x.shape
    tm = min(512, s)
    blk = pl.BlockSpec((1, tm, d), lambda bi, i: (bi, i, 0))
    return pl.pallas_call(
        _residual_kernel,
        grid=(b, s // tm),
        in_specs=[blk, blk, pl.BlockSpec((1, 1, d), lambda bi, i: (bi, 0, 0))],
        out_specs=blk,
        out_shape=jax.ShapeDtypeStruct((b, s, d), F32),
        compiler_params=_cparams(("parallel", "parallel")),
        name="gated_residual",
    )(x, y, gate.reshape(b, 1, d))


def _rope_tables(positions):
    inv_freq = ROPE_THETA ** (-jnp.arange(0, ROT_DIM, 2, dtype=F32) / ROT_DIM)
    ang = positions.astype(F32)[..., None] * inv_freq
    cos = jnp.cos(ang)
    sin = jnp.sin(ang)
    b, s, _ = ang.shape
    rest = DA_QK_DIM - ROT_DIM
    cos64 = jnp.concatenate([cos, cos, jnp.ones((b, s, rest), F32)], axis=-1)
    sin64 = jnp.concatenate([sin, sin, jnp.zeros((b, s, rest), F32)], axis=-1)
    return jnp.tile(cos64, (1, 1, LANES // DA_QK_DIM)), jnp.tile(sin64, (1, 1, LANES // DA_QK_DIM))


def kernel(x, c, positions, norm_mix_g, norm_ffn_g, w_ada, b_ada, w_in, q_norm_g, k_norm_g, lam_q1, lam_k1, lam_q2, lam_k2, subln_g, conv_w, conv_b, dt_bias, a_log, d_skip, ssd_norm_g, w_out, peer_wq, peer_k1, peer_k2, peer_u, peer_v):
    depth = w_ada.shape[0]
    b, s, d = x.shape
    cos_t, sin_t = _rope_tables(positions)
    mod = adaln(c, w_ada, b_ada)
    for i in range(depth):
        lam_init = 0.8 - 0.6 * math.exp(-0.3 * i)
        sh_m, sc_m, g_m, sh_f, sc_f, g_f = [mod[i, :, j * d:(j + 1) * d] for j in range(6)]
        w_in_p = jnp.pad(w_in[i], ((0, 0), (0, IN_DIM_PAD - IN_DIM))).astype(BF16)
        proj = in_proj(x, norm_mix_g[i], sh_m, sc_m, w_in_p)
        qn, kn, vn = qk_prep(proj, cos_t, sin_t, q_norm_g[i], k_norm_g[i])
        lam = (jnp.exp(jnp.sum(lam_q1[i] * lam_k1[i])) - jnp.exp(jnp.sum(lam_q2[i] * lam_k2[i])) + lam_init)
        attn = diff_attention(qn, kn, vn, lam, subln_g[i], lam_init)
        ssd = ssd_group(proj, conv_w[i], conv_b[i], dt_bias[i], a_log[i], d_skip[i], ssd_norm_g[i])
        x = out_proj(attn, ssd, w_out[i].astype(BF16), x, g_m)
        h2, idx, gate = peer_route(x, norm_ffn_g[i], sh_f, sc_f, peer_wq[i].astype(BF16),
                                   peer_k1[i].astype(BF16), peer_k2[i].astype(BF16))
        n_exp = peer_u.shape[1]
        uv = jnp.concatenate([peer_u[i].astype(BF16).reshape(n_exp, _ROW_TILES, LANES),
                              peer_v[i].astype(BF16).reshape(n_exp, _ROW_TILES, LANES)], axis=1)
        y3 = peer_experts(idx.reshape(b * s, _NSEL), gate.reshape(b * s, _NSEL),
                          h2.reshape(b * s, _ROW_TILES, LANES), uv)
        x = gated_residual(x, y3.reshape(b, s, d), g_f)
    return x
```

```python
import functools
import math

import jax
import jax.numpy as jnp
from jax import lax
from jax.experimental import pallas as pl
from jax.experimental.pallas import tpu as pltpu

F32 = jnp.float32
BF16 = jnp.bfloat16

D_MODEL = 2048
DA_WIDTH = 1024
DA_HEADS = 8
DA_V_DIM = 128
DA_QK_DIM = 64
ROT_DIM = 16
ROPE_THETA = 500000.0
SSD_INNER = 1024
SSD_HEADS = 16
SSD_HEAD_DIM = 64
SSD_GROUPS = 2
SSD_STATE = 128
SSD_CHUNK = 128
CONV_WIDTH = 4
IN_DIM = 5648
PEER_HEADS = 8
PEER_N_KEYS = 128
PEER_HALF = 128
PEER_TOPK = 16
RMS_EPS = 1e-6

LANES = 128
SUBLANES = 8
IN_DIM_PAD = 5760
VMEM_LIMIT = 52 * 1024 * 1024

_Z_BLK = 3
_XS_BLK = 4
_BC_BLK = 10
_DT_BLK = 44

NEG_BIG = -1e30


def _cparams(sem):
    return pltpu.CompilerParams(dimension_semantics=sem, vmem_limit_bytes=VMEM_LIMIT)


def _adaln_kernel(c_ref, w_ref, b_ref, o_ref):
    c = c_ref[...]
    cond = (c * jax.nn.sigmoid(c)).astype(BF16)
    w = w_ref[0].astype(BF16)
    o_ref[0] = jnp.dot(cond, w, preferred_element_type=F32) + b_ref[0]


def adaln(c, w_ada, b_ada):
    depth, d, n = w_ada.shape
    b = c.shape[0]
    tn = 1024
    return pl.pallas_call(
        _adaln_kernel,
        grid=(depth, n // tn),
        in_specs=[
            pl.BlockSpec((b, d), lambda l, j: (0, 0)),
            pl.BlockSpec((1, d, tn), lambda l, j: (l, 0, j)),
            pl.BlockSpec((1, 1, tn), lambda l, j: (l, 0, j)),
        ],
        out_specs=pl.BlockSpec((1, b, tn), lambda l, j: (l, 0, j)),
        out_shape=jax.ShapeDtypeStruct((depth, b, n), F32),
        compiler_params=_cparams(("parallel", "parallel")),
        name="adaln",
    )(c, w_ada, b_ada.reshape(depth, 1, n))


def _norm_modulate(x, g, shift, scale):
    ms = jnp.mean(x * x, axis=-1, keepdims=True)
    y = x * lax.rsqrt(ms + RMS_EPS) * g
    return y * (1.0 + scale) + shift


def _in_proj_kernel(x_ref, g_ref, sh_ref, sc_ref, w_ref, o_ref, h_scr):
    @pl.when(pl.program_id(2) == 0)
    def _():
        h = _norm_modulate(x_ref[0], g_ref[...], sh_ref[0], sc_ref[0])
        h_scr[...] = h.astype(BF16)

    o_ref[0] = jnp.dot(h_scr[...], w_ref[...], preferred_element_type=F32)


def in_proj(x, g, shift, scale, w):
    b, s, d = x.shape
    n = w.shape[1]
    tm = min(512, s)
    tn = 1152
    return pl.pallas_call(
        _in_proj_kernel,
        grid=(b, s // tm, n // tn),
        in_specs=[
            pl.BlockSpec((1, tm, d), lambda bi, i, j: (bi, i, 0)),
            pl.BlockSpec((1, d), lambda bi, i, j: (0, 0)),
            pl.BlockSpec((1, 1, d), lambda bi, i, j: (bi, 0, 0)),
            pl.BlockSpec((1, 1, d), lambda bi, i, j: (bi, 0, 0)),
            pl.BlockSpec((d, tn), lambda bi, i, j: (0, j)),
        ],
        out_specs=pl.BlockSpec((1, tm, tn), lambda bi, i, j: (bi, i, j)),
        out_shape=jax.ShapeDtypeStruct((b, s, n), F32),
        scratch_shapes=[pltpu.VMEM((tm, d), BF16)],
        compiler_params=_cparams(("parallel", "parallel", "arbitrary")),
        name="in_proj",
    )(x, g.reshape(1, d), shift.reshape(b, 1, d), scale.reshape(b, 1, d), w)


def _qk_prep_kernel(q_ref, k_ref, v_ref, cos_ref, sin_ref, gq_ref, gk_ref, seg_ref, rot_ref,
                    qo_ref, ko_ref, vo_ref):
    cos = cos_ref[0]
    sin = sin_ref[0]
    seg = seg_ref[...]
    rot_m = rot_ref[...]

    def prep(x_ref, g, o_ref, out_scale):
        for c in range(DA_WIDTH // LANES):
            sl = slice(c * LANES, (c + 1) * LANES)
            x = x_ref[0, :, sl]
            ss = jnp.dot((x * x).astype(BF16), seg, preferred_element_type=F32)
            y = x * lax.rsqrt(ss * (1.0 / DA_QK_DIM) + RMS_EPS) * g
            r = jnp.dot(y.astype(BF16), rot_m, preferred_element_type=F32)
            o_ref[0, :, sl] = ((y * cos + r * sin) * out_scale).astype(BF16)

    prep(q_ref, gq_ref[...], qo_ref, DA_QK_DIM ** -0.5)
    prep(k_ref, gk_ref[...], ko_ref, 1.0)
    vo_ref[0] = v_ref[0].astype(BF16)


def qk_prep(proj, cos_t, sin_t, gq, gk):
    b, s, _ = proj.shape
    ts = min(512, s)
    lane = jnp.arange(LANES)
    seg_m = (lane[:, None] // DA_QK_DIM == lane[None, :] // DA_QK_DIM).astype(BF16)
    off = lane % DA_QK_DIM
    half = ROT_DIM // 2
    src = lane[:, None]
    dst = lane[None, :]
    rot_m = (jnp.where((off[None, :] < half) & (src == dst + half), -1.0, 0.0)
             + jnp.where((off[None, :] >= half) & (off[None, :] < ROT_DIM) & (src == dst - half), 1.0, 0.0)
             ).astype(BF16)
    gq_t = jnp.tile(gq, LANES // DA_QK_DIM).reshape(1, LANES)
    gk_t = jnp.tile(gk, LANES // DA_QK_DIM).reshape(1, LANES)
    blk = lambda c: pl.BlockSpec((1, ts, DA_WIDTH), lambda bi, i, c=c: (bi, i, c))
    const = lambda shape: pl.BlockSpec(shape, lambda bi, i: (0,) * len(shape))
    out_sds = jax.ShapeDtypeStruct((b, s, DA_WIDTH), BF16)
    return pl.pallas_call(
        _qk_prep_kernel,
        grid=(b, s // ts),
        in_specs=[blk(0), blk(1), blk(2),
                  pl.BlockSpec((1, ts, LANES), lambda bi, i: (bi, i, 0)),
                  pl.BlockSpec((1, ts, LANES), lambda bi, i: (bi, i, 0)),
                  const((1, LANES)), const((1, LANES)),
                  const((LANES, LANES)), const((LANES, LANES))],
        out_specs=[pl.BlockSpec((1, ts, DA_WIDTH), lambda bi, i: (bi, i, 0))] * 3,
        out_shape=[out_sds] * 3,
        compiler_params=_cparams(("parallel", "parallel")),
        name="qk_prep",
    )(proj, proj, proj, cos_t, sin_t, gq_t, gk_t, seg_m, rot_m)


def _attn_kernel(lam_ref, q_ref, k_ref, v_ref, g_ref, o_ref, *, tq, tk, out_scale):
    qi = pl.program_id(2)
    q = q_ref[0]
    lane = lax.broadcasted_iota(jnp.int32, (tq, LANES), 1)
    zero = jnp.zeros_like(q)
    q0 = jnp.where(lane < DA_QK_DIM, q, zero)
    q1 = jnp.where(lane >= DA_QK_DIM, q, zero)
    row = qi * tq + lax.broadcasted_iota(jnp.int32, (tq, tk), 0)
    col0 = lax.broadcasted_iota(jnp.int32, (tq, tk), 1)
    nt = (((1,), (1,)), ((), ()))

    def update(s, m, l, acc, v):
        m_new = jnp.maximum(m, jnp.max(s, axis=-1, keepdims=True))
        p = jnp.exp(s - m_new)
        alpha = jnp.exp(m - m_new)
        l = alpha * l + jnp.sum(p, axis=-1, keepdims=True)
        acc = alpha * acc + jnp.dot(p.astype(BF16), v, preferred_element_type=F32)
        return m_new, l, acc

    def body(j, carry):
        m0, l0, a0, m1, l1, a1 = carry
        start = pl.multiple_of(j * tk, tk)
        k = k_ref[0, pl.ds(start, tk), :]
        v = v_ref[0, pl.ds(start, tk), :]
        ok = (col0 + j * tk) <= row
        s0 = jnp.where(ok, lax.dot_general(q0, k, nt, preferred_element_type=F32), NEG_BIG)
        s1 = jnp.where(ok, lax.dot_general(q1, k, nt, preferred_element_type=F32), NEG_BIG)
        m0, l0, a0 = update(s0, m0, l0, a0, v)
        m1, l1, a1 = update(s1, m1, l1, a1, v)
        return m0, l0, a0, m1, l1, a1

    mi = jnp.full((tq, 1), NEG_BIG, F32)
    li = jnp.zeros((tq, 1), F32)
    ai = jnp.zeros((tq, DA_V_DIM), F32)
    n_kv = ((qi + 1) * tq + tk - 1) // tk
    m0, l0, a0, m1, l1, a1 = lax.fori_loop(0, n_kv, body, (mi, li, ai, mi, li, ai))
    out = a0 / l0 - lam_ref[0] * (a1 / l1)
    ms = jnp.mean(out * out, axis=-1, keepdims=True)
    o_ref[0] = (out * lax.rsqrt(ms + RMS_EPS) * g_ref[...] * out_scale).astype(BF16)


def diff_attention(qn, kn, vn, lam, subln_g, lam_init):
    b, s, _ = qn.shape
    tq = min(512, s)
    tk = min(256, s)
    kern = functools.partial(_attn_kernel, tq=tq, tk=tk, out_scale=1.0 - lam_init)
    return pl.pallas_call(
        kern,
        grid=(b, DA_HEADS, s // tq),
        in_specs=[
            pl.BlockSpec(memory_space=pltpu.SMEM),
            pl.BlockSpec((1, tq, LANES), lambda bi, h, i: (bi, i, h)),
            pl.BlockSpec((1, s, LANES), lambda bi, h, i: (bi, 0, h)),
            pl.BlockSpec((1, s, LANES), lambda bi, h, i: (bi, 0, h)),
            pl.BlockSpec((1, LANES), lambda bi, h, i: (0, 0)),
        ],
        out_specs=pl.BlockSpec((1, tq, LANES), lambda bi, h, i: (bi, i, h)),
        out_shape=jax.ShapeDtypeStruct((b, s, DA_WIDTH), BF16),
        compiler_params=_cparams(("parallel", "parallel", "parallel")),
        name="diff_attn",
    )(lam.reshape(1).astype(F32), qn, kn, vn, subln_g.reshape(1, LANES))


def _split_bf16(x):
    hi = x.astype(BF16)
    lo = (x - hi.astype(F32)).astype(BF16)
    return hi, lo


def _silu(x):
    return x * jax.nn.sigmoid(x)


def _ssd_kernel(z_ref, xs_ref, bc_ref, dt_ref, cwx_ref, cwb_ref, cbx_ref, cbb_ref, dtb_ref,
                alog_ref, dsk_ref, ng_ref, tri_ref, exp_ref, o_ref, pxs, pbc, st):
    q = SSD_CHUNK

    @pl.when(pl.program_id(1) == 0)
    def _():
        pxs[...] = jnp.zeros_like(pxs)
        pbc[...] = jnp.zeros_like(pbc)
        st[...] = jnp.zeros_like(st)

    def conv(x, prev, w_ref, b_ref):
        acc = x * w_ref[CONV_WIDTH - 1:CONV_WIDTH, :] + b_ref[...]
        row8 = lax.broadcasted_iota(jnp.int32, (SUBLANES, x.shape[1]), 0)
        for k in range(1, CONV_WIDTH):
            xr = pltpu.roll(x, k, 0)
            pr = pltpu.roll(prev, k, 0)
            first = jnp.where(row8 < k, pr, xr[0:SUBLANES])
            xk = jnp.concatenate([first, xr[SUBLANES:]], axis=0)
            acc = acc + xk * w_ref[CONV_WIDTH - 1 - k:CONV_WIDTH - k, :]
        return acc

    xs_raw = xs_ref[0]
    bc_raw = bc_ref[0]
    xs = _silu(conv(xs_raw, pxs[...], cwx_ref, cbx_ref))
    bcs = _silu(conv(bc_raw, pbc[...], cwb_ref, cbb_ref))
    pxs[...] = xs_raw[q - SUBLANES:q]
    pbc[...] = bc_raw[q - SUBLANES:q]

    gw = SSD_GROUPS * SSD_STATE
    bm = [bcs[:, g * SSD_STATE:(g + 1) * SSD_STATE] for g in range(SSD_GROUPS)]
    cm = [bcs[:, gw + g * SSD_STATE:gw + (g + 1) * SSD_STATE].astype(BF16) for g in range(SSD_GROUPS)]

    v = dt_ref[0] + dtb_ref[...]
    dt = jnp.maximum(v, 0.0) + jnp.log1p(jnp.exp(-jnp.abs(v)))
    a_dt = dt * (-jnp.exp(alog_ref[...]))
    tri = tri_ref[...]
    a_hi, a_lo = _split_bf16(a_dt)
    a_cs = (jnp.dot(tri, a_hi, preferred_element_type=F32)
            + jnp.dot(tri, a_lo, preferred_element_type=F32))
    a_cs_t = a_cs.T
    exp_a = jnp.exp(a_cs)
    decay = jnp.exp(a_cs[q - 1:q, :] - a_cs)
    stack = jnp.concatenate([dt, exp_a, decay], axis=0)
    s_hi, s_lo = _split_bf16(stack)
    ex = (jnp.dot(s_hi, exp_ref[...], preferred_element_type=F32)
          + jnp.dot(s_lo, exp_ref[...], preferred_element_type=F32))
    dt_e = ex[0:q]
    exp_a_e = ex[q:2 * q]
    decay_e = ex[2 * q:3 * q]
    x_dt = xs * dt_e
    x_dec = x_dt * decay_e

    nt = (((1,), (1,)), ((), ()))
    cb = [lax.dot_general(cm[g], bm[g].astype(BF16), nt, preferred_element_type=F32)
          for g in range(SSD_GROUPS)]
    b_t = [bm[g].T.astype(BF16) for g in range(SSD_GROUPS)]
    causal = (lax.broadcasted_iota(jnp.int32, (q, q), 0) >= lax.broadcasted_iota(jnp.int32, (q, q), 1))
    lane = lax.broadcasted_iota(jnp.int32, (q, LANES), 1)
    first_head = lane < SSD_HEAD_DIM
    z = z_ref[0]
    heads_per_group = SSD_HEADS // SSD_GROUPS
    pairs = []
    for j in range(SSD_HEADS // 2):
        sl = slice(j * LANES, (j + 1) * LANES)
        g = (2 * j) // heads_per_group
        xp = x_dt[:, sl]
        y = jnp.zeros((q, LANES), F32)
        for hh in range(2):
            h = 2 * j + hh
            seg = a_cs[:, h:h + 1] - a_cs_t[h:h + 1, :]
            decay_l = jnp.exp(jnp.where(causal, seg, -jnp.inf))
            gm = (cb[g] * decay_l).astype(BF16)
            keep = first_head if hh == 0 else jnp.logical_not(first_head)
            xm = jnp.where(keep, xp, 0.0).astype(BF16)
            y = y + jnp.dot(gm, xm, preferred_element_type=F32)
        sp = st[j]
        y_off = jnp.dot(cm[g], sp.astype(BF16), preferred_element_type=F32) * exp_a_e[:, sl]
        st[j] = (sp * exp_a_e[q - 1:q, sl]
                 + jnp.dot(b_t[g], x_dec[:, sl].astype(BF16), preferred_element_type=F32))
        yp = y + y_off + dsk_ref[:, sl] * xs[:, sl]
        pairs.append(yp * _silu(z[:, sl]))
    y = jnp.concatenate(pairs, axis=1)
    gwid = SSD_INNER // SSD_GROUPS
    outs = []
    for g in range(SSD_GROUPS):
        yg = y[:, g * gwid:(g + 1) * gwid]
        ms = jnp.mean(yg * yg, axis=-1, keepdims=True)
        outs.append(yg * lax.rsqrt(ms + RMS_EPS) * ng_ref[:, g * gwid:(g + 1) * gwid])
    o_ref[0] = jnp.concatenate(outs, axis=1).astype(BF16)


def ssd_group(proj, conv_w, conv_b, dt_bias, a_log, d_skip, norm_g):
    b, s, _ = proj.shape
    q = SSD_CHUNK
    pad = lambda vec: jnp.pad(vec, (0, LANES - SSD_HEADS)).reshape(1, LANES)
    idx = jnp.arange(q)
    tri = (idx[:, None] >= idx[None, :]).astype(BF16)
    expand = (jnp.arange(LANES)[:, None] == (jnp.arange(SSD_INNER)[None, :] // SSD_HEAD_DIM)).astype(BF16)
    bcw = 2 * SSD_GROUPS * SSD_STATE
    const = lambda shape: pl.BlockSpec(shape, lambda bi, c: (0,) * len(shape))
    return pl.pallas_call(
        _ssd_kernel,
        grid=(b, s // q),
        in_specs=[
            pl.BlockSpec((1, q, SSD_INNER), lambda bi, c: (bi, c, _Z_BLK)),
            pl.BlockSpec((1, q, SSD_INNER), lambda bi, c: (bi, c, _XS_BLK)),
            pl.BlockSpec((1, q, bcw), lambda bi, c: (bi, c, _BC_BLK)),
            pl.BlockSpec((1, q, LANES), lambda bi, c: (bi, c, _DT_BLK)),
            const((CONV_WIDTH, SSD_INNER)), const((CONV_WIDTH, bcw)),
            const((1, SSD_INNER)), const((1, bcw)),
            const((1, LANES)), const((1, LANES)),
            const((1, SSD_INNER)), const((1, SSD_INNER)),
            const((q, q)), const((LANES, SSD_INNER)),
        ],
        out_specs=pl.BlockSpec((1, q, SSD_INNER), lambda bi, c: (bi, c, 0)),
        out_shape=jax.ShapeDtypeStruct((b, s, SSD_INNER), BF16),
        scratch_shapes=[pltpu.VMEM((SUBLANES, SSD_INNER), F32),
                        pltpu.VMEM((SUBLANES, bcw), F32),
                        pltpu.VMEM((SSD_HEADS // 2, SSD_STATE, LANES), F32)],
        compiler_params=_cparams(("parallel", "arbitrary")),
        name="ssd",
    )(proj, proj, proj, proj,
      conv_w[:, :SSD_INNER], conv_w[:, SSD_INNER:],
      conv_b[:SSD_INNER].reshape(1, -1), conv_b[SSD_INNER:].reshape(1, -1),
      pad(dt_bias), pad(a_log),
      jnp.repeat(d_skip, SSD_HEAD_DIM).reshape(1, -1), norm_g.reshape(1, -1),
      tri, expand)


def _out_proj_kernel(a_ref, s_ref, w_ref, x_ref, g_ref, o_ref):
    acc = jnp.dot(a_ref[0], w_ref[0:DA_WIDTH, :], preferred_element_type=F32)
    acc = acc + jnp.dot(s_ref[0], w_ref[DA_WIDTH:, :], preferred_element_type=F32)
    o_ref[0] = x_ref[0] + g_ref[0] * acc


def out_proj(attn, ssd, w, x, gate):
    b, s, d = x.shape
    tm = min(512, s)
    return pl.pallas_call(
        _out_proj_kernel,
        grid=(b, s // tm),
        in_specs=[
            pl.BlockSpec((1, tm, DA_WIDTH), lambda bi, i: (bi, i, 0)),
            pl.BlockSpec((1, tm, SSD_INNER), lambda bi, i: (bi, i, 0)),
            pl.BlockSpec((d, d), lambda bi, i: (0, 0)),
            pl.BlockSpec((1, tm, d), lambda bi, i: (bi, i, 0)),
            pl.BlockSpec((1, 1, d), lambda bi, i: (bi, 0, 0)),
        ],
        out_specs=pl.BlockSpec((1, tm, d), lambda bi, i: (bi, i, 0)),
        out_shape=jax.ShapeDtypeStruct((b, s, d), F32),
        compiler_params=_cparams(("parallel", "parallel")),
        name="out_proj",
    )(attn, ssd, w, x, gate.reshape(b, 1, d))


def _extract_topk(vals, pos, payload, k):
    n = vals.shape[1]
    rowk = lax.broadcasted_iota(jnp.int32, (k, n), 0)
    top_v = jnp.zeros((k, n), F32)
    top_p = jnp.zeros((k, n), F32)
    big = jnp.float32(1e9)
    for r in range(k):
        m = jnp.max(vals, axis=0, keepdims=True)
        p = jnp.min(jnp.where(vals == m, pos, big), axis=0, keepdims=True)
        sel = pos == p
        if payload is None:
            pay = p
        else:
            pay = jnp.max(jnp.where(sel, payload, -1.0), axis=0, keepdims=True)
        top_v = jnp.where(rowk == r, m, top_v)
        top_p = jnp.where(rowk == r, pay, top_p)
        vals = jnp.where(sel, -jnp.inf, vals)
    return top_v, top_p


def _route_kernel(x_ref, g_ref, sh_ref, sc_ref, wq_ref, k1_ref, k2_ref, h_ref, idx_ref, gate_ref):
    tm = x_ref.shape[1]
    h = _norm_modulate(x_ref[0], g_ref[...], sh_ref[0], sc_ref[0])
    h_ref[0] = h
    qv = jnp.dot(h.astype(BF16), wq_ref[...], preferred_element_type=F32)
    nt = (((1,), (1,)), ((), ()))
    key_pos = lax.broadcasted_iota(jnp.int32, (PEER_N_KEYS, tm), 0).astype(F32)
    k = PEER_TOPK
    sub = lax.broadcasted_iota(jnp.int32, (SUBLANES, tm), 0).astype(F32)
    sub16 = lax.broadcasted_iota(jnp.int32, (k, tm), 0).astype(F32)
    idx_rows = []
    gate_rows = []
    for hd in range(PEER_HEADS):
        base = hd * 2 * PEER_HALF
        qa = qv[:, base:base + PEER_HALF].astype(BF16)
        qb = qv[:, base + PEER_HALF:base + 2 * PEER_HALF].astype(BF16)
        s1 = lax.dot_general(k1_ref[hd], qa, nt, preferred_element_type=F32)
        s2 = lax.dot_general(k2_ref[hd], qb, nt, preferred_element_type=F32)
        v1, i1 = _extract_topk(s1, key_pos, None, k)
        v2, i2 = _extract_topk(s2, key_pos, None, k)
        cv = [v1[0:1] + v2]
        ci = [i1[0:1] * PEER_N_KEYS + i2]
        cp = [sub16]
        for a in range(1, SUBLANES):
            cv.append(v1[a:a + 1] + v2[0:SUBLANES])
            ci.append(i1[a:a + 1] * PEER_N_KEYS + i2[0:SUBLANES])
            cp.append(sub + float(a * k))
        cv.append(v1[SUBLANES:k] + v2[0:1])
        ci.append(i1[SUBLANES:k] * PEER_N_KEYS + i2[0:1])
        cp.append((sub + float(SUBLANES)) * float(k))
        cand_v = jnp.concatenate(cv, axis=0)
        cand_i = jnp.concatenate(ci, axis=0)
        cand_p = jnp.concatenate(cp, axis=0)
        top, eidx = _extract_topk(cand_v, cand_p, cand_i, k)
        e = jnp.exp(top - jnp.max(top, axis=0, keepdims=True))
        gate_rows.append(e / jnp.sum(e, axis=0, keepdims=True))
        idx_rows.append(eidx)
    idx_ref[0] = jnp.concatenate(idx_rows, axis=0).T.astype(jnp.int32)
    gate_ref[0] = jnp.concatenate(gate_rows, axis=0).T


def peer_route(x, g, shift, scale, wq, k1, k2):
    b, s, d = x.shape
    tm = min(256, s)
    nsel = PEER_HEADS * PEER_TOPK
    const = lambda shape: pl.BlockSpec(shape, lambda bi, i: (0,) * len(shape))
    return pl.pallas_call(
        _route_kernel,
        grid=(b, s // tm),
        in_specs=[
            pl.BlockSpec((1, tm, d), lambda bi, i: (bi, i, 0)),
            const((1, d)),
            pl.BlockSpec((1, 1, d), lambda bi, i: (bi, 0, 0)),
            pl.BlockSpec((1, 1, d), lambda bi, i: (bi, 0, 0)),
            const((d, d)),
            const((PEER_HEADS, PEER_N_KEYS, PEER_HALF)),
            const((PEER_HEADS, PEER_N_KEYS, PEER_HALF)),
        ],
        out_specs=[pl.BlockSpec((1, tm, d), lambda bi, i: (bi, i, 0)),
                   pl.BlockSpec((1, tm, nsel), lambda bi, i: (bi, i, 0)),
                   pl.BlockSpec((1, tm, nsel), lambda bi, i: (bi, i, 0))],
        out_shape=[jax.ShapeDtypeStruct((b, s, d), F32),
                   jax.ShapeDtypeStruct((b, s, nsel), jnp.int32),
                   jax.ShapeDtypeStruct((b, s, nsel), F32)],
        compiler_params=_cparams(("parallel", "parallel")),
        name="peer_route",
    )(x, g.reshape(1, d), shift.reshape(b, 1, d), scale.reshape(b, 1, d), wq, k1, k2)


_ROW_TILES = D_MODEL // LANES
_NSEL = PEER_HEADS * PEER_TOPK
_GROUP = 4
_GROUPS_IN_FLIGHT = 4
_PACK = 2 * SUBLANES
_BURST = 16
_VSUM = 4


def _expert_kernel(idx_ref, gate_ref, h_ref, sel_ref, uv_hbm, o_ref, buf, sem, pscr, splat):
    te = h_ref.shape[0]
    n_groups = te // _GROUP
    ones = jnp.ones((_PACK, LANES), BF16)
    nt = (((1,), (1,)), ((), ()))

    def row_copies(g):
        slot = lax.rem(g, _GROUPS_IN_FLIGHT)
        starts = []
        for i in range(_GROUP):
            t = g * _GROUP + i
            for k in range(_NSEL):
                starts.append(functools.partial(
                    lambda t, i, k: pltpu.make_async_copy(
                        uv_hbm.at[idx_ref[t, k]], buf.at[slot, i, k], sem.at[slot]).start(priority=k % 2),
                    t, i, k))
        return starts

    def issue(g):
        for start in row_copies(g):
            start()

    def wait(g):
        slot = lax.rem(g, _GROUPS_IN_FLIGHT)
        for i in range(_GROUP):
            pltpu.make_async_copy(uv_hbm.at[pl.ds(0, _NSEL)], buf.at[slot, i], sem.at[slot]).wait()

    def start_next(pending, n):
        for _ in range(n):
            start = next(pending, None)
            if start is not None:
                start()

    def u_side(g, pending=iter(())):
        slot = lax.rem(g, _GROUPS_IN_FLIGHT)
        par = lax.rem(g, 2)
        for i in range(_GROUP):
            t = g * _GROUP + i
            hb = h_ref[t].astype(BF16)
            for kk in range(_NSEL // 2):
                parts = []
                for k in (2 * kk, 2 * kk + 1):
                    pr = (buf[slot, i, k, 0:_ROW_TILES, :] * hb).astype(F32)
                    parts.append(pr[0:SUBLANES] + pr[SUBLANES:_ROW_TILES])
                pscr[i, kk * _PACK:(kk + 1) * _PACK, :] = jnp.concatenate(parts, axis=0).astype(BF16)
                if kk % _BURST == _BURST - 1:
                    start_next(pending, _BURST)
            m = jnp.dot(sel_ref[...], pscr[i], preferred_element_type=F32)
            m_hi, m_lo = _split_bf16(m)
            a = (lax.dot_general(ones, m_hi, nt, preferred_element_type=F32)
                 + lax.dot_general(ones, m_lo, nt, preferred_element_type=F32))[0:1]
            act = 0.5 * a * (1.0 + lax.erf(a * (2.0 ** -0.5))) * gate_ref[pl.ds(t, 1), :]
            hi = lax.shift_right_logical(pltpu.bitcast(act.astype(BF16).astype(F32), jnp.uint32), jnp.uint32(16))
            word = pltpu.bitcast(lax.shift_left(hi, jnp.uint32(16)) | hi, F32)
            splat[par, i] = jnp.broadcast_to(word, (LANES, LANES)).T

    def v_side(g, pending=iter(())):
        slot = lax.rem(g, _GROUPS_IN_FLIGHT)
        par = lax.rem(g, 2)
        for i in range(_GROUP):
            t = g * _GROUP + i
            accs = [jnp.zeros((_ROW_TILES, LANES), F32) for _ in range(2)]
            for kq in range(_NSEL // _VSUM):
                part = None
                for k in range(kq * _VSUM, (kq + 1) * _VSUM):
                    w = pltpu.bitcast(jnp.broadcast_to(splat[par, i, pl.ds(k, 1), :], (SUBLANES, LANES)), BF16)
                    term = w * buf[slot, i, k, _ROW_TILES:2 * _ROW_TILES, :]
                    part = term if part is None else part + term
                accs[kq % 2] = accs[kq % 2] + part.astype(F32)
                if kq % (2 * _BURST // _VSUM) == 2 * _BURST // _VSUM - 1:
                    start_next(pending, _BURST)
            o_ref[t] = accs[0] + accs[1]

    ahead = _GROUPS_IN_FLIGHT - 1
    for g0 in range(ahead):
        issue(g0)
    wait(0)
    u_side(0)

    def steady(g, carry):
        pending = iter(row_copies(g + ahead))
        wait(g + 1)
        u_side(g + 1, pending)
        v_side(g, pending)
        start_next(pending, _GROUP * _NSEL)
        return carry

    zero = 0 * pl.program_id(0)
    n_steady = n_groups - ahead
    lax.fori_loop(zero, zero + n_steady, steady, 0)

    def drain(g, carry):
        wait(g + 1)
        u_side(g + 1)
        v_side(g)
        return carry

    lax.fori_loop(zero + n_steady, zero + (n_groups - 1), drain, 0)
    v_side(zero + (n_groups - 1))


def peer_experts(idx, gate, h3, uv):
    m = idx.shape[0]
    te = min(128, m)
    sel = (jnp.arange(_NSEL)[:, None] == jnp.arange(_NSEL * SUBLANES)[None, :] // SUBLANES).astype(BF16)
    return pl.pallas_call(
        _expert_kernel,
        grid=(m // te,),
        in_specs=[
            pl.BlockSpec((te, _NSEL), lambda i: (i, 0), memory_space=pltpu.SMEM),
            pl.BlockSpec((te, _NSEL), lambda i: (i, 0)),
            pl.BlockSpec((te, _ROW_TILES, LANES), lambda i: (i, 0, 0)),
            pl.BlockSpec((_NSEL, _NSEL * SUBLANES), lambda i: (0, 0)),
            pl.BlockSpec(memory_space=pl.ANY),
        ],
        out_specs=pl.BlockSpec((te, _ROW_TILES, LANES), lambda i: (i, 0, 0)),
        out_shape=jax.ShapeDtypeStruct((m, _ROW_TILES, LANES), F32),
        scratch_shapes=[
            pltpu.VMEM((_GROUPS_IN_FLIGHT, _GROUP, _NSEL, 2 * _ROW_TILES, LANES), BF16),
            pltpu.SemaphoreType.DMA((_GROUPS_IN_FLIGHT,)),
            pltpu.VMEM((_GROUP, _NSEL * SUBLANES, LANES), BF16),
            pltpu.VMEM((2, _GROUP, LANES, LANES), F32),
        ],
        compiler_params=_cparams(("arbitrary",)),
        name="peer_experts",
    )(idx, gate, h3, sel, uv)


def _residual_kernel(x_ref, y_ref, g_ref, o_ref):
    o_ref[0] = x_ref[0] + g_ref[0] * y_ref[0]


def gated_residual(x, y, gate):
    b, s, d = x.shape
    tm = min(512, s)
    blk = pl.BlockSpec((1, tm, d), lambda bi, i: (bi, i, 0))
    return pl.pallas_call(
        _residual_kernel,
        grid=(b, s // tm),
        in_specs=[blk, blk, pl.BlockSpec((1, 1, d), lambda bi, i: (bi, 0, 0))],
        out_specs=blk,
        out_shape=jax.ShapeDtypeStruct((b, s, d), F32),
        compiler_params=_cparams(("parallel", "parallel")),
        name="gated_residual",
    )(x, y, gate.reshape(b, 1, d))


def _rope_tables(positions):
    inv_freq = ROPE_THETA ** (-jnp.arange(0, ROT_DIM, 2, dtype=F32) / ROT_DIM)
    ang = positions.astype(F32)[..., None] * inv_freq
    cos = jnp.cos(ang)
    sin = jnp.sin(ang)
    b, s, _ = ang.shape
    rest = DA_QK_DIM - ROT_DIM
    cos64 = jnp.concatenate([cos, cos, jnp.ones((b, s, rest), F32)], axis=-1)
    sin64 = jnp.concatenate([sin, sin, jnp.zeros((b, s, rest), F32)], axis=-1)
    return jnp.tile(cos64, (1, 1, LANES // DA_QK_DIM)), jnp.tile(sin64, (1, 1, LANES // DA_QK_DIM))


def kernel(x, c, positions, norm_mix_g, norm_ffn_g, w_ada, b_ada, w_in, q_norm_g, k_norm_g, lam_q1, lam_k1, lam_q2, lam_k2, subln_g, conv_w, conv_b, dt_bias, a_log, d_skip, ssd_norm_g, w_out, peer_wq, peer_k1, peer_k2, peer_u, peer_v):
    depth = w_ada.shape[0]
    b, s, d = x.shape
    cos_t, sin_t = _rope_tables(positions)
    mod = adaln(c, w_ada, b_ada)
    for i in range(depth):
        lam_init = 0.8 - 0.6 * math.exp(-0.3 * i)
        sh_m, sc_m, g_m, sh_f, sc_f, g_f = [mod[i, :, j * d:(j + 1) * d] for j in range(6)]
        w_in_p = jnp.pad(w_in[i], ((0, 0), (0, IN_DIM_PAD - IN_DIM))).astype(BF16)
        proj = in_proj(x, norm_mix_g[i], sh_m, sc_m, w_in_p)
        qn, kn, vn = qk_prep(proj, cos_t, sin_t, q_norm_g[i], k_norm_g[i])
        lam = (jnp.exp(jnp.sum(lam_q1[i] * lam_k1[i])) - jnp.exp(jnp.sum(lam_q2[i] * lam_k2[i])) + lam_init)
        attn = diff_attention(qn, kn, vn, lam, subln_g[i], lam_init)
        ssd = ssd_group(proj, conv_w[i], conv_b[i], dt_bias[i], a_log[i], d_skip[i], ssd_norm_g[i])
        x = out_proj(attn, ssd, w_out[i].astype(BF16), x, g_m)
        h2, idx, gate = peer_route(x, norm_ffn_g[i], sh_f, sc_f, peer_wq[i].astype(BF16),
                                   peer_k1[i].astype(BF16), peer_k2[i].astype(BF16))
        n_exp = peer_u.shape[1]
        uv = jnp.concatenate([peer_u[i].astype(BF16).reshape(n_exp, _ROW_TILES, LANES),
                              peer_v[i].astype(BF16).reshape(n_exp, _ROW_TILES, LANES)], axis=1)
        y3 = peer_experts(idx.reshape(b * s, _NSEL), gate.reshape(b * s, _NSEL),
                          h2.reshape(b * s, _ROW_TILES, LANES), uv)
        x = gated_residual(x, y3.reshape(b, s, d), g_f)
    return x
```

```python
import functools
import math

import jax
import jax.numpy as jnp
from jax import lax
from jax.experimental import pallas as pl
from jax.experimental.pallas import tpu as pltpu

F32 = jnp.float32
BF16 = jnp.bfloat16

D_MODEL = 2048
DA_WIDTH = 1024
DA_HEADS = 8
DA_V_DIM = 128
DA_QK_DIM = 64
ROT_DIM = 16
ROPE_THETA = 500000.0
SSD_INNER = 1024
SSD_HEADS = 16
SSD_HEAD_DIM = 64
SSD_GROUPS = 2
SSD_STATE = 128
SSD_CHUNK = 128
CONV_WIDTH = 4
IN_DIM = 5648
PEER_HEADS = 8
PEER_N_KEYS = 128
PEER_HALF = 128
PEER_TOPK = 16
RMS_EPS = 1e-6

LANES = 128
SUBLANES = 8
IN_DIM_PAD = 5760
VMEM_LIMIT = 52 * 1024 * 1024

_Z_BLK = 3
_XS_BLK = 4
_BC_BLK = 10
_DT_BLK = 44

NEG_BIG = -1e30


def _cparams(sem):
    return pltpu.CompilerParams(dimension_semantics=sem, vmem_limit_bytes=VMEM_LIMIT)


def _adaln_kernel(c_ref, w_ref, b_ref, o_ref):
    c = c_ref[...]
    cond = (c * jax.nn.sigmoid(c)).astype(BF16)
    w = w_ref[0].astype(BF16)
    o_ref[0] = jnp.dot(cond, w, preferred_element_type=F32) + b_ref[0]


def adaln(c, w_ada, b_ada):
    depth, d, n = w_ada.shape
    b = c.shape[0]
    tn = 1024
    return pl.pallas_call(
        _adaln_kernel,
        grid=(depth, n // tn),
        in_specs=[
            pl.BlockSpec((b, d), lambda l, j: (0, 0)),
            pl.BlockSpec((1, d, tn), lambda l, j: (l, 0, j)),
            pl.BlockSpec((1, 1, tn), lambda l, j: (l, 0, j)),
        ],
        out_specs=pl.BlockSpec((1, b, tn), lambda l, j: (l, 0, j)),
        out_shape=jax.ShapeDtypeStruct((depth, b, n), F32),
        compiler_params=_cparams(("parallel", "parallel")),
        name="adaln",
    )(c, w_ada, b_ada.reshape(depth, 1, n))


def _norm_modulate(x, g, shift, scale):
    ms = jnp.mean(x * x, axis=-1, keepdims=True)
    y = x * lax.rsqrt(ms + RMS_EPS) * g
    return y * (1.0 + scale) + shift


def _in_proj_kernel(x_ref, g_ref, sh_ref, sc_ref, w_ref, o_ref, h_scr):
    @pl.when(pl.program_id(2) == 0)
    def _():
        h = _norm_modulate(x_ref[0], g_ref[...], sh_ref[0], sc_ref[0])
        h_scr[...] = h.astype(BF16)

    o_ref[0] = jnp.dot(h_scr[...], w_ref[...], preferred_element_type=F32)


def in_proj(x, g, shift, scale, w):
    b, s, d = x.shape
    n = w.shape[1]
    tm = min(512, s)
    tn = 1152
    return pl.pallas_call(
        _in_proj_kernel,
        grid=(b, s // tm, n // tn),
        in_specs=[
            pl.BlockSpec((1, tm, d), lambda bi, i, j: (bi, i, 0)),
            pl.BlockSpec((1, d), lambda bi, i, j: (0, 0)),
            pl.BlockSpec((1, 1, d), lambda bi, i, j: (bi, 0, 0)),
            pl.BlockSpec((1, 1, d), lambda bi, i, j: (bi, 0, 0)),
            pl.BlockSpec((d, tn), lambda bi, i, j: (0, j)),
        ],
        out_specs=pl.BlockSpec((1, tm, tn), lambda bi, i, j: (bi, i, j)),
        out_shape=jax.ShapeDtypeStruct((b, s, n), F32),
        scratch_shapes=[pltpu.VMEM((tm, d), BF16)],
        compiler_params=_cparams(("parallel", "parallel", "arbitrary")),
        name="in_proj",
    )(x, g.reshape(1, d), shift.reshape(b, 1, d), scale.reshape(b, 1, d), w)


def _qk_prep_kernel(q_ref, k_ref, v_ref, cos_ref, sin_ref, gq_ref, gk_ref, seg_ref, rot_ref,
                    qo_ref, ko_ref, vo_ref):
    cos = cos_ref[0]
    sin = sin_ref[0]
    seg = seg_ref[...]
    rot_m = rot_ref[...]

    def prep(x_ref, g, o_ref, out_scale):
        for c in range(DA_WIDTH // LANES):
            sl = slice(c * LANES, (c + 1) * LANES)
            x = x_ref[0, :, sl]
            ss = jnp.dot((x * x).astype(BF16), seg, preferred_element_type=F32)
            y = x * lax.rsqrt(ss * (1.0 / DA_QK_DIM) + RMS_EPS) * g
            r = jnp.dot(y.astype(BF16), rot_m, preferred_element_type=F32)
            o_ref[0, :, sl] = ((y * cos + r * sin) * out_scale).astype(BF16)

    prep(q_ref, gq_ref[...], qo_ref, DA_QK_DIM ** -0.5)
    prep(k_ref, gk_ref[...], ko_ref, 1.0)
    vo_ref[0] = v_ref[0].astype(BF16)


def qk_prep(proj, cos_t, sin_t, gq, gk):
    b, s, _ = proj.shape
    ts = min(512, s)
    lane = jnp.arange(LANES)
    seg_m = (lane[:, None] // DA_QK_DIM == lane[None, :] // DA_QK_DIM).astype(BF16)
    off = lane % DA_QK_DIM
    half = ROT_DIM // 2
    src = lane[:, None]
    dst = lane[None, :]
    rot_m = (jnp.where((off[None, :] < half) & (src == dst + half), -1.0, 0.0)
             + jnp.where((off[None, :] >= half) & (off[None, :] < ROT_DIM) & (src == dst - half), 1.0, 0.0)
             ).astype(BF16)
    gq_t = jnp.tile(gq, LANES // DA_QK_DIM).reshape(1, LANES)
    gk_t = jnp.tile(gk, LANES // DA_QK_DIM).reshape(1, LANES)
    blk = lambda c: pl.BlockSpec((1, ts, DA_WIDTH), lambda bi, i, c=c: (bi, i, c))
    const = lambda shape: pl.BlockSpec(shape, lambda bi, i: (0,) * len(shape))
    out_sds = jax.ShapeDtypeStruct((b, s, DA_WIDTH), BF16)
    return pl.pallas_call(
        _qk_prep_kernel,
        grid=(b, s // ts),
        in_specs=[blk(0), blk(1), blk(2),
                  pl.BlockSpec((1, ts, LANES), lambda bi, i: (bi, i, 0)),
                  pl.BlockSpec((1, ts, LANES), lambda bi, i: (bi, i, 0)),
                  const((1, LANES)), const((1, LANES)),
                  const((LANES, LANES)), const((LANES, LANES))],
        out_specs=[pl.BlockSpec((1, ts, DA_WIDTH), lambda bi, i: (bi, i, 0))] * 3,
        out_shape=[out_sds] * 3,
        compiler_params=_cparams(("parallel", "parallel")),
        name="qk_prep",
    )(proj, proj, proj, cos_t, sin_t, gq_t, gk_t, seg_m, rot_m)


def _attn_kernel(lam_ref, q_ref, k_ref, v_ref, g_ref, o_ref, *, tq, tk, out_scale):
    qi = pl.program_id(2)
    q = q_ref[0]
    lane = lax.broadcasted_iota(jnp.int32, (tq, LANES), 1)
    zero = jnp.zeros_like(q)
    q0 = jnp.where(lane < DA_QK_DIM, q, zero)
    q1 = jnp.where(lane >= DA_QK_DIM, q, zero)
    row = qi * tq + lax.broadcasted_iota(jnp.int32, (tq, tk), 0)
    col0 = lax.broadcasted_iota(jnp.int32, (tq, tk), 1)
    nt = (((1,), (1,)), ((), ()))

    def update(s, m, l, acc, v):
        m_new = jnp.maximum(m, jnp.max(s, axis=-1, keepdims=True))
        p = jnp.exp(s - m_new)
        alpha = jnp.exp(m - m_new)
        l = alpha * l + jnp.sum(p, axis=-1, keepdims=True)
        acc = alpha * acc + jnp.dot(p.astype(BF16), v, preferred_element_type=F32)
        return m_new, l, acc

    def body(j, carry):
        m0, l0, a0, m1, l1, a1 = carry
        start = pl.multiple_of(j * tk, tk)
        k = k_ref[0, pl.ds(start, tk), :]
        v = v_ref[0, pl.ds(start, tk), :]
        ok = (col0 + j * tk) <= row
        s0 = jnp.where(ok, lax.dot_general(q0, k, nt, preferred_element_type=F32), NEG_BIG)
        s1 = jnp.where(ok, lax.dot_general(q1, k, nt, preferred_element_type=F32), NEG_BIG)
        m0, l0, a0 = update(s0, m0, l0, a0, v)
        m1, l1, a1 = update(s1, m1, l1, a1, v)
        return m0, l0, a0, m1, l1, a1

    mi = jnp.full((tq, 1), NEG_BIG, F32)
    li = jnp.zeros((tq, 1), F32)
    ai = jnp.zeros((tq, DA_V_DIM), F32)
    n_kv = ((qi + 1) * tq + tk - 1) // tk
    m0, l0, a0, m1, l1, a1 = lax.fori_loop(0, n_kv, body, (mi, li, ai, mi, li, ai))
    out = a0 / l0 - lam_ref[0] * (a1 / l1)
    ms = jnp.mean(out * out, axis=-1, keepdims=True)
    o_ref[0] = (out * lax.rsqrt(ms + RMS_EPS) * g_ref[...] * out_scale).astype(BF16)


def diff_attention(qn, kn, vn, lam, subln_g, lam_init):
    b, s, _ = qn.shape
    tq = min(512, s)
    tk = min(256, s)
    kern = functools.partial(_attn_kernel, tq=tq, tk=tk, out_scale=1.0 - lam_init)
    return pl.pallas_call(
        kern,
        grid=(b, DA_HEADS, s // tq),
        in_specs=[
            pl.BlockSpec(memory_space=pltpu.SMEM),
            pl.BlockSpec((1, tq, LANES), lambda bi, h, i: (bi, i, h)),
            pl.BlockSpec((1, s, LANES), lambda bi, h, i: (bi, 0, h)),
            pl.BlockSpec((1, s, LANES), lambda bi, h, i: (bi, 0, h)),
            pl.BlockSpec((1, LANES), lambda bi, h, i: (0, 0)),
        ],
        out_specs=pl.BlockSpec((1, tq, LANES), lambda bi, h, i: (bi, i, h)),
        out_shape=jax.ShapeDtypeStruct((b, s, DA_WIDTH), BF16),
        compiler_params=_cparams(("parallel", "parallel", "parallel")),
        name="diff_attn",
    )(lam.reshape(1).astype(F32), qn, kn, vn, subln_g.reshape(1, LANES))


def _split_bf16(x):
    hi = x.astype(BF16)
    lo = (x - hi.astype(F32)).astype(BF16)
    return hi, lo


def _silu(x):
    return x * jax.nn.sigmoid(x)


def _ssd_kernel(z_ref, xs_ref, bc_ref, dt_ref, cwx_ref, cwb_ref, cbx_ref, cbb_ref, dtb_ref,
                alog_ref, dsk_ref, ng_ref, tri_ref, exp_ref, o_ref, pxs, pbc, st):
    q = SSD_CHUNK

    @pl.when(pl.program_id(1) == 0)
    def _():
        pxs[...] = jnp.zeros_like(pxs)
        pbc[...] = jnp.zeros_like(pbc)
        st[...] = jnp.zeros_like(st)

    def conv(x, prev, w_ref, b_ref):
        acc = x * w_ref[CONV_WIDTH - 1:CONV_WIDTH, :] + b_ref[...]
        row8 = lax.broadcasted_iota(jnp.int32, (SUBLANES, x.shape[1]), 0)
        for k in range(1, CONV_WIDTH):
            xr = pltpu.roll(x, k, 0)
            pr = pltpu.roll(prev, k, 0)
            first = jnp.where(row8 < k, pr, xr[0:SUBLANES])
            xk = jnp.concatenate([first, xr[SUBLANES:]], axis=0)
            acc = acc + xk * w_ref[CONV_WIDTH - 1 - k:CONV_WIDTH - k, :]
        return acc

    xs_raw = xs_ref[0]
    bc_raw = bc_ref[0]
    xs = _silu(conv(xs_raw, pxs[...], cwx_ref, cbx_ref))
    bcs = _silu(conv(bc_raw, pbc[...], cwb_ref, cbb_ref))
    pxs[...] = xs_raw[q - SUBLANES:q]
    pbc[...] = bc_raw[q - SUBLANES:q]

    gw = SSD_GROUPS * SSD_STATE
    bm = [bcs[:, g * SSD_STATE:(g + 1) * SSD_STATE] for g in range(SSD_GROUPS)]
    cm = [bcs[:, gw + g * SSD_STATE:gw + (g + 1) * SSD_STATE].astype(BF16) for g in range(SSD_GROUPS)]

    v = dt_ref[0] + dtb_ref[...]
    dt = jnp.maximum(v, 0.0) + jnp.log1p(jnp.exp(-jnp.abs(v)))
    a_dt = dt * (-jnp.exp(alog_ref[...]))
    tri = tri_ref[...]
    a_hi, a_lo = _split_bf16(a_dt)
    a_cs = (jnp.dot(tri, a_hi, preferred_element_type=F32)
            + jnp.dot(tri, a_lo, preferred_element_type=F32))
    a_cs_t = a_cs.T
    exp_a = jnp.exp(a_cs)
    decay = jnp.exp(a_cs[q - 1:q, :] - a_cs)
    stack = jnp.concatenate([dt, exp_a, decay], axis=0)
    s_hi, s_lo = _split_bf16(stack)
    ex = (jnp.dot(s_hi, exp_ref[...], preferred_element_type=F32)
          + jnp.dot(s_lo, exp_ref[...], preferred_element_type=F32))
    dt_e = ex[0:q]
    exp_a_e = ex[q:2 * q]
    decay_e = ex[2 * q:3 * q]
    x_dt = xs * dt_e
    x_dec = x_dt * decay_e

    nt = (((1,), (1,)), ((), ()))
    cb = [lax.dot_general(cm[g], bm[g].astype(BF16), nt, preferred_element_type=F32)
          for g in range(SSD_GROUPS)]
    b_t = [bm[g].T.astype(BF16) for g in range(SSD_GROUPS)]
    causal = (lax.broadcasted_iota(jnp.int32, (q, q), 0) >= lax.broadcasted_iota(jnp.int32, (q, q), 1))
    lane = lax.broadcasted_iota(jnp.int32, (q, LANES), 1)
    first_head = lane < SSD_HEAD_DIM
    z = z_ref[0]
    heads_per_group = SSD_HEADS // SSD_GROUPS
    pairs = []
    for j in range(SSD_HEADS // 2):
        sl = slice(j * LANES, (j + 1) * LANES)
        g = (2 * j) // heads_per_group
        xp = x_dt[:, sl]
        y = jnp.zeros((q, LANES), F32)
        for hh in range(2):
            h = 2 * j + hh
            seg = a_cs[:, h:h + 1] - a_cs_t[h:h + 1, :]
            decay_l = jnp.exp(jnp.where(causal, seg, -jnp.inf))
            gm = (cb[g] * decay_l).astype(BF16)
            keep = first_head if hh == 0 else jnp.logical_not(first_head)
            xm = jnp.where(keep, xp, 0.0).astype(BF16)
            y = y + jnp.dot(gm, xm, preferred_element_type=F32)
        sp = st[j]
        y_off = jnp.dot(cm[g], sp.astype(BF16), preferred_element_type=F32) * exp_a_e[:, sl]
        st[j] = (sp * exp_a_e[q - 1:q, sl]
                 + jnp.dot(b_t[g], x_dec[:, sl].astype(BF16), preferred_element_type=F32))
        yp = y + y_off + dsk_ref[:, sl] * xs[:, sl]
        pairs.append(yp * _silu(z[:, sl]))
    y = jnp.concatenate(pairs, axis=1)
    gwid = SSD_INNER // SSD_GROUPS
    outs = []
    for g in range(SSD_GROUPS):
        yg = y[:, g * gwid:(g + 1) * gwid]
        ms = jnp.mean(yg * yg, axis=-1, keepdims=True)
        outs.append(yg * lax.rsqrt(ms + RMS_EPS) * ng_ref[:, g * gwid:(g + 1) * gwid])
    o_ref[0] = jnp.concatenate(outs, axis=1).astype(BF16)


def ssd_group(proj, conv_w, conv_b, dt_bias, a_log, d_skip, norm_g):
    b, s, _ = proj.shape
    q = SSD_CHUNK
    pad = lambda vec: jnp.pad(vec, (0, LANES - SSD_HEADS)).reshape(1, LANES)
    idx = jnp.arange(q)
    tri = (idx[:, None] >= idx[None, :]).astype(BF16)
    expand = (jnp.arange(LANES)[:, None] == (jnp.arange(SSD_INNER)[None, :] // SSD_HEAD_DIM)).astype(BF16)
    bcw = 2 * SSD_GROUPS * SSD_STATE
    const = lambda shape: pl.BlockSpec(shape, lambda bi, c: (0,) * len(shape))
    return pl.pallas_call(
        _ssd_kernel,
        grid=(b, s // q),
        in_specs=[
            pl.BlockSpec((1, q, SSD_INNER), lambda bi, c: (bi, c, _Z_BLK)),
            pl.BlockSpec((1, q, SSD_INNER), lambda bi, c: (bi, c, _XS_BLK)),
            pl.BlockSpec((1, q, bcw), lambda bi, c: (bi, c, _BC_BLK)),
            pl.BlockSpec((1, q, LANES), lambda bi, c: (bi, c, _DT_BLK)),
            const((CONV_WIDTH, SSD_INNER)), const((CONV_WIDTH, bcw)),
            const((1, SSD_INNER)), const((1, bcw)),
            const((1, LANES)), const((1, LANES)),
            const((1, SSD_INNER)), const((1, SSD_INNER)),
            const((q, q)), const((LANES, SSD_INNER)),
        ],
        out_specs=pl.BlockSpec((1, q, SSD_INNER), lambda bi, c: (bi, c, 0)),
        out_shape=jax.ShapeDtypeStruct((b, s, SSD_INNER), BF16),
        scratch_shapes=[pltpu.VMEM((SUBLANES, SSD_INNER), F32),
                        pltpu.VMEM((SUBLANES, bcw), F32),
                        pltpu.VMEM((SSD_HEADS // 2, SSD_STATE, LANES), F32)],
        compiler_params=_cparams(("parallel", "arbitrary")),
        name="ssd",
    )(proj, proj, proj, proj,
      conv_w[:, :SSD_INNER], conv_w[:, SSD_INNER:],
      conv_b[:SSD_INNER].reshape(1, -1), conv_b[SSD_INNER:].reshape(1, -1),
      pad(dt_bias), pad(a_log),
      jnp.repeat(d_skip, SSD_HEAD_DIM).reshape(1, -1), norm_g.reshape(1, -1),
      tri, expand)


def _out_proj_kernel(a_ref, s_ref, w_ref, x_ref, g_ref, o_ref):
    acc = jnp.dot(a_ref[0], w_ref[0:DA_WIDTH, :], preferred_element_type=F32)
    acc = acc + jnp.dot(s_ref[0], w_ref[DA_WIDTH:, :], preferred_element_type=F32)
    o_ref[0] = x_ref[0] + g_ref[0] * acc


def out_proj(attn, ssd, w, x, gate):
    b, s, d = x.shape
    tm = min(512, s)
    return pl.pallas_call(
        _out_proj_kernel,
        grid=(b, s // tm),
        in_specs=[
            pl.BlockSpec((1, tm, DA_WIDTH), lambda bi, i: (bi, i, 0)),
            pl.BlockSpec((1, tm, SSD_INNER), lambda bi, i: (bi, i, 0)),
            pl.BlockSpec((d, d), lambda bi, i: (0, 0)),
            pl.BlockSpec((1, tm, d), lambda bi, i: (bi, i, 0)),
            pl.BlockSpec((1, 1, d), lambda bi, i: (bi, 0, 0)),
        ],
        out_specs=pl.BlockSpec((1, tm, d), lambda bi, i: (bi, i, 0)),
        out_shape=jax.ShapeDtypeStruct((b, s, d), F32),
        compiler_params=_cparams(("parallel", "parallel")),
        name="out_proj",
    )(attn, ssd, w, x, gate.reshape(b, 1, d))


def _extract_topk(vals, pos, payload, k):
    n = vals.shape[1]
    rowk = lax.broadcasted_iota(jnp.int32, (k, n), 0)
    top_v = jnp.zeros((k, n), F32)
    top_p = jnp.zeros((k, n), F32)
    big = jnp.float32(1e9)
    for r in range(k):
        m = jnp.max(vals, axis=0, keepdims=True)
        p = jnp.min(jnp.where(vals == m, pos, big), axis=0, keepdims=True)
        sel = pos == p
        if payload is None:
            pay = p
        else:
            pay = jnp.max(jnp.where(sel, payload, -1.0), axis=0, keepdims=True)
        top_v = jnp.where(rowk == r, m, top_v)
        top_p = jnp.where(rowk == r, pay, top_p)
        vals = jnp.where(sel, -jnp.inf, vals)
    return top_v, top_p


def _route_kernel(x_ref, g_ref, sh_ref, sc_ref, wq_ref, k1_ref, k2_ref, h_ref, idx_ref, gate_ref):
    tm = x_ref.shape[1]
    h = _norm_modulate(x_ref[0], g_ref[...], sh_ref[0], sc_ref[0])
    h_ref[0] = h
    qv = jnp.dot(h.astype(BF16), wq_ref[...], preferred_element_type=F32)
    nt = (((1,), (1,)), ((), ()))
    key_pos = lax.broadcasted_iota(jnp.int32, (PEER_N_KEYS, tm), 0).astype(F32)
    k = PEER_TOPK
    sub = lax.broadcasted_iota(jnp.int32, (SUBLANES, tm), 0).astype(F32)
    sub16 = lax.broadcasted_iota(jnp.int32, (k, tm), 0).astype(F32)
    idx_rows = []
    gate_rows = []
    for hd in range(PEER_HEADS):
        base = hd * 2 * PEER_HALF
        qa = qv[:, base:base + PEER_HALF].astype(BF16)
        qb = qv[:, base + PEER_HALF:base + 2 * PEER_HALF].astype(BF16)
        s1 = lax.dot_general(k1_ref[hd], qa, nt, preferred_element_type=F32)
        s2 = lax.dot_general(k2_ref[hd], qb, nt, preferred_element_type=F32)
        v1, i1 = _extract_topk(s1, key_pos, None, k)
        v2, i2 = _extract_topk(s2, key_pos, None, k)
        cv = [v1[0:1] + v2]
        ci = [i1[0:1] * PEER_N_KEYS + i2]
        cp = [sub16]
        for a in range(1, SUBLANES):
            cv.append(v1[a:a + 1] + v2[0:SUBLANES])
            ci.append(i1[a:a + 1] * PEER_N_KEYS + i2[0:SUBLANES])
            cp.append(sub + float(a * k))
        cv.append(v1[SUBLANES:k] + v2[0:1])
        ci.append(i1[SUBLANES:k] * PEER_N_KEYS + i2[0:1])
        cp.append((sub + float(SUBLANES)) * float(k))
        cand_v = jnp.concatenate(cv, axis=0)
        cand_i = jnp.concatenate(ci, axis=0)
        cand_p = jnp.concatenate(cp, axis=0)
        top, eidx = _extract_topk(cand_v, cand_p, cand_i, k)
        e = jnp.exp(top - jnp.max(top, axis=0, keepdims=True))
        gate_rows.append(e / jnp.sum(e, axis=0, keepdims=True))
        idx_rows.append(eidx)
    idx_ref[0] = jnp.concatenate(idx_rows, axis=0).T.astype(jnp.int32)
    gate_ref[0] = jnp.concatenate(gate_rows, axis=0).T


def peer_route(x, g, shift, scale, wq, k1, k2):
    b, s, d = x.shape
    tm = min(256, s)
    nsel = PEER_HEADS * PEER_TOPK
    const = lambda shape: pl.BlockSpec(shape, lambda bi, i: (0,) * len(shape))
    return pl.pallas_call(
        _route_kernel,
        grid=(b, s // tm),
        in_specs=[
            pl.BlockSpec((1, tm, d), lambda bi, i: (bi, i, 0)),
            const((1, d)),
            pl.BlockSpec((1, 1, d), lambda bi, i: (bi, 0, 0)),
            pl.BlockSpec((1, 1, d), lambda bi, i: (bi, 0, 0)),
            const((d, d)),
            const((PEER_HEADS, PEER_N_KEYS, PEER_HALF)),
            const((PEER_HEADS, PEER_N_KEYS, PEER_HALF)),
        ],
        out_specs=[pl.BlockSpec((1, tm, d), lambda bi, i: (bi, i, 0)),
                   pl.BlockSpec((1, tm, nsel), lambda bi, i: (bi, i, 0)),
                   pl.BlockSpec((1, tm, nsel), lambda bi, i: (bi, i, 0))],
        out_shape=[jax.ShapeDtypeStruct((b, s, d), F32),
                   jax.ShapeDtypeStruct((b, s, nsel), jnp.int32),
                   jax.ShapeDtypeStruct((b, s, nsel), F32)],
        compiler_params=_cparams(("parallel", "parallel")),
        name="peer_route",
    )(x, g.reshape(1, d), shift.reshape(b, 1, d), scale.reshape(b, 1, d), wq, k1, k2)


_ROW_TILES = D_MODEL // LANES
_NSEL = PEER_HEADS * PEER_TOPK
_GROUP = 4
_GROUPS_IN_FLIGHT = 4
_PACK = 2 * SUBLANES
_BURST = 8
_VSUM = 4


def _expert_kernel(idx_ref, gate_ref, h_ref, sel_ref, uv_hbm, o_ref, buf, sem, pscr, splat):
    te = h_ref.shape[0]
    n_groups = te // _GROUP
    ones = jnp.ones((_PACK, LANES), BF16)
    nt = (((1,), (1,)), ((), ()))

    def row_copies(g, slot):
        starts = []
        for i in range(_GROUP):
            t = g * _GROUP + i
            for k in range(_NSEL):
                starts.append(functools.partial(
                    lambda t, i, k: pltpu.make_async_copy(
                        uv_hbm.at[idx_ref[t, k]], buf.at[slot, i, k], sem.at[slot]).start(priority=k % 2),
                    t, i, k))
        return starts

    def wait(slot):
        for i in range(_GROUP):
            pltpu.make_async_copy(uv_hbm.at[pl.ds(0, _NSEL)], buf.at[slot, i], sem.at[slot]).wait()

    def start_next(pending, n):
        for _ in range(n):
            start = next(pending, None)
            if start is not None:
                start()

    def u_products(g, slot, i, pending):
        t = g * _GROUP + i
        hb = h_ref[t].astype(BF16)
        for kk in range(_NSEL // 2):
            parts = []
            for k in (2 * kk, 2 * kk + 1):
                pr = (buf[slot, i, k, 0:_ROW_TILES, :] * hb).astype(F32)
                parts.append(pr[0:SUBLANES] + pr[SUBLANES:_ROW_TILES])
            pscr[i, kk * _PACK:(kk + 1) * _PACK, :] = jnp.concatenate(parts, axis=0).astype(BF16)
            if kk % _BURST == _BURST - 1:
                start_next(pending, _BURST)
        return jnp.dot(sel_ref[...], pscr[i], preferred_element_type=F32)

    def u_finish(g, i, m):
        t = g * _GROUP + i
        m_hi, m_lo = _split_bf16(m)
        a = (lax.dot_general(ones, m_hi, nt, preferred_element_type=F32)
             + lax.dot_general(ones, m_lo, nt, preferred_element_type=F32))[0:1]
        act = 0.5 * a * (1.0 + lax.erf(a * (2.0 ** -0.5))) * gate_ref[pl.ds(t, 1), :]
        hi = lax.shift_right_logical(pltpu.bitcast(act.astype(BF16).astype(F32), jnp.uint32), jnp.uint32(16))
        return pltpu.bitcast(lax.shift_left(hi, jnp.uint32(16)) | hi, F32)

    def v_token(g, slot, i, pending):
        t = g * _GROUP + i
        accs = [jnp.zeros((_ROW_TILES, LANES), F32) for _ in range(2)]
        for kq in range(_NSEL // _VSUM):
            part = None
            for k in range(kq * _VSUM, (kq + 1) * _VSUM):
                w = pltpu.bitcast(jnp.broadcast_to(splat[i, pl.ds(k, 1), :], (SUBLANES, LANES)), BF16)
                term = w * buf[slot, i, k, _ROW_TILES:2 * _ROW_TILES, :]
                part = term if part is None else part + term
            accs[kq % 2] = accs[kq % 2] + part.astype(F32)
            if kq % (2 * _BURST // _VSUM) == 2 * _BURST // _VSUM - 1:
                start_next(pending, _BURST)
        o_ref[t] = accs[0] + accs[1]

    def step(g_u, slot_u, g_v, slot_v, words, pending=iter(())):
        if words is not None:
            for i in range(_GROUP):
                splat[i] = jnp.broadcast_to(words[i], (LANES, LANES)).T
        nxt = []
        for i in range(_GROUP):
            m = u_products(g_u, slot_u, i, pending) if g_u is not None else None
            if words is not None:
                v_token(g_v, slot_v, i, pending)
            if m is not None:
                nxt.append(u_finish(g_u, i, m))
        return tuple(nxt)

    nif = _GROUPS_IN_FLIGHT
    ahead = nif - 1
    for g0 in range(ahead):
        for start in row_copies(g0, g0):
            start()
    wait(0)
    words0 = step(0, 0, None, None, None)

    def steady(g, words):
        pending = iter(row_copies(g + ahead, lax.rem(g + ahead, nif)))
        r1 = lax.rem(g + 1, nif)
        wait(r1)
        nxt = step(g + 1, r1, g, lax.rem(g, nif), words, pending)
        start_next(pending, _GROUP * _NSEL)
        return nxt

    def drain(g, words):
        r1 = lax.rem(g + 1, nif)
        wait(r1)
        return step(g + 1, r1, g, lax.rem(g, nif), words)

    zero = 0 * pl.program_id(0)
    n_steady = n_groups - ahead
    words = lax.fori_loop(zero, zero + n_steady, steady, words0)
    words = lax.fori_loop(zero + n_steady, zero + (n_groups - 1), drain, words)
    step(None, None, n_groups - 1, (n_groups - 1) % nif, words)


def peer_experts(idx, gate, h3, uv):
    m = idx.shape[0]
    te = min(256, m)
    sel =(jnp.arange(_NSEL)[:, None] == jnp.arange(_NSEL * SUBLANES)[None, :] // SUBLANES).astype(BF16)
    return pl.pallas_call(
        _expert_kernel,
        grid=(m // te,),
        in_specs=[
            pl.BlockSpec((te, _NSEL), lambda i: (i, 0), memory_space=pltpu.SMEM),
            pl.BlockSpec((te, _NSEL), lambda i: (i, 0)),
            pl.BlockSpec((te, _ROW_TILES, LANES), lambda i: (i, 0, 0)),
            pl.BlockSpec((_NSEL, _NSEL * SUBLANES), lambda i: (0, 0)),
            pl.BlockSpec(memory_space=pl.ANY),
        ],
        out_specs=pl.BlockSpec((te, _ROW_TILES, LANES), lambda i: (i, 0, 0)),
        out_shape=jax.ShapeDtypeStruct((m, _ROW_TILES, LANES), F32),
        scratch_shapes=[
            pltpu.VMEM((_GROUPS_IN_FLIGHT, _GROUP, _NSEL, 2 * _ROW_TILES, LANES), BF16),
            pltpu.SemaphoreType.DMA((_GROUPS_IN_FLIGHT,)),
            pltpu.VMEM((_GROUP, _NSEL * SUBLANES, LANES), BF16),
            pltpu.VMEM((_GROUP, LANES, LANES), F32),
        ],
        compiler_params=_cparams(("arbitrary",)),
        name="peer_experts",
    )(idx, gate, h3, sel, uv)


def _residual_kernel(x_ref, y_ref, g_ref, o_ref):
    o_ref[0] = x_ref[0] + g_ref[0] * y_ref[0]


def gated_residual(x, y, gate):
    b, s, d = x.shape
    tm = min(512, s)
    blk = pl.BlockSpec((1, tm, d), lambda bi, i: (bi, i, 0))
    return pl.pallas_call(
        _residual_kernel,
        grid=(b, s // tm),
        in_specs=[blk, blk, pl.BlockSpec((1, 1, d), lambda bi, i: (bi, 0, 0))],
        out_specs=blk,
        out_shape=jax.ShapeDtypeStruct((b, s, d), F32),
        compiler_params=_cparams(("parallel", "parallel")),
        name="gated_residual",
    )(x, y, gate.reshape(b, 1, d))


def _rope_tables(positions):
    inv_freq = ROPE_THETA ** (-jnp.arange(0, ROT_DIM, 2, dtype=F32) / ROT_DIM)
    ang = positions.astype(F32)[..., None] * inv_freq
    cos = jnp.cos(ang)
    sin = jnp.sin(ang)
    b, s, _ = ang.shape
    rest = DA_QK_DIM - ROT_DIM
    cos64 = jnp.concatenate([cos, cos, jnp.ones((b, s, rest), F32)], axis=-1)
    sin64 = jnp.concatenate([sin, sin, jnp.zeros((b, s, rest), F32)], axis=-1)
    return jnp.tile(cos64, (1, 1, LANES // DA_QK_DIM)), jnp.tile(sin64, (1, 1, LANES // DA_QK_DIM))


def kernel(x, c, positions, norm_mix_g, norm_ffn_g, w_ada, b_ada, w_in, q_norm_g, k_norm_g, lam_q1, lam_k1, lam_q2, lam_k2, subln_g, conv_w, conv_b, dt_bias, a_log, d_skip, ssd_norm_g, w_out, peer_wq, peer_k1, peer_k2, peer_u, peer_v):
    depth = w_ada.shape[0]
    b, s, d = x.shape
    cos_t, sin_t = _rope_tables(positions)
    mod = adaln(c, w_ada, b_ada)
    for i in range(depth):
        lam_init = 0.8 - 0.6 * math.exp(-0.3 * i)
        sh_m, sc_m, g_m, sh_f, sc_f, g_f = [mod[i, :, j * d:(j + 1) * d] for j in range(6)]
        w_in_p = jnp.pad(w_in[i], ((0, 0), (0, IN_DIM_PAD - IN_DIM))).astype(BF16)
        proj = in_proj(x, norm_mix_g[i], sh_m, sc_m, w_in_p)
        qn, kn, vn = qk_prep(proj, cos_t, sin_t, q_norm_g[i], k_norm_g[i])
        lam = (jnp.exp(jnp.sum(lam_q1[i] * lam_k1[i])) - jnp.exp(jnp.sum(lam_q2[i] * lam_k2[i])) + lam_init)
        attn = diff_attention(qn, kn, vn, lam, subln_g[i], lam_init)
        ssd = ssd_group(proj, conv_w[i], conv_b[i], dt_bias[i], a_log[i], d_skip[i], ssd_norm_g[i])
        x = out_proj(attn, ssd, w_out[i].astype(BF16), x, g_m)
        h2, idx, gate = peer_route(x, norm_ffn_g[i], sh_f, sc_f, peer_wq[i].astype(BF16),
                                   peer_k1[i].astype(BF16), peer_k2[i].astype(BF16))
        n_exp = peer_u.shape[1]
        uv = jnp.concatenate([peer_u[i].astype(BF16).reshape(n_exp, _ROW_TILES, LANES),
                              peer_v[i].astype(BF16).reshape(n_exp, _ROW_TILES, LANES)], axis=1)
        y3 = peer_experts(idx.reshape(b * s, _NSEL), gate.reshape(b * s, _NSEL),
                          h2.reshape(b * s, _ROW_TILES, LANES), uv)
        x = gated_residual(x, y3.reshape(b, s, d), g_f)
    return x
```

```python
import functools
import math

import jax
import jax.numpy as jnp
from jax import lax
from jax.experimental import pallas as pl
from jax.experimental.pallas import tpu as pltpu

F32 = jnp.float32
BF16 = jnp.bfloat16

D_MODEL = 2048
DA_WIDTH = 1024
DA_HEADS = 8
DA_V_DIM = 128
DA_QK_DIM = 64
ROT_DIM = 16
ROPE_THETA = 500000.0
SSD_INNER = 1024
SSD_HEADS = 16
SSD_HEAD_DIM = 64
SSD_GROUPS = 2
SSD_STATE = 128
SSD_CHUNK = 128
CONV_WIDTH = 4
IN_DIM = 5648
PEER_HEADS = 8
PEER_N_KEYS = 128
PEER_HALF = 128
PEER_TOPK = 16
RMS_EPS = 1e-6

LANES = 128
SUBLANES = 8
IN_DIM_PAD = 5760
VMEM_LIMIT = 52 * 1024 * 1024

_Z_BLK = 3
_XS_BLK = 4
_BC_BLK = 10
_DT_BLK = 44

NEG_BIG = -1e30


def _cparams(sem):
    return pltpu.CompilerParams(dimension_semantics=sem, vmem_limit_bytes=VMEM_LIMIT)


def _adaln_kernel(c_ref, w_ref, b_ref, o_ref):
    c = c_ref[...]
    cond = (c * jax.nn.sigmoid(c)).astype(BF16)
    w = w_ref[0].astype(BF16)
    o_ref[0] = jnp.dot(cond, w, preferred_element_type=F32) + b_ref[0]


def adaln(c, w_ada, b_ada):
    depth, d, n = w_ada.shape
    b = c.shape[0]
    tn = 1024
    return pl.pallas_call(
        _adaln_kernel,
        grid=(depth, n // tn),
        in_specs=[
            pl.BlockSpec((b, d), lambda l, j: (0, 0)),
            pl.BlockSpec((1, d, tn), lambda l, j: (l, 0, j)),
            pl.BlockSpec((1, 1, tn), lambda l, j: (l, 0, j)),
        ],
        out_specs=pl.BlockSpec((1, b, tn), lambda l, j: (l, 0, j)),
        out_shape=jax.ShapeDtypeStruct((depth, b, n), F32),
        compiler_params=_cparams(("parallel", "parallel")),
        name="adaln",
    )(c, w_ada, b_ada.reshape(depth, 1, n))


def _norm_modulate(x, g, shift, scale):
    ms = jnp.mean(x * x, axis=-1, keepdims=True)
    y = x * lax.rsqrt(ms + RMS_EPS) * g
    return y * (1.0 + scale) + shift


def _in_proj_kernel(x_ref, g_ref, sh_ref, sc_ref, w_ref, o_ref, h_scr):
    @pl.when(pl.program_id(2) == 0)
    def _():
        h = _norm_modulate(x_ref[0], g_ref[...], sh_ref[0], sc_ref[0])
        h_scr[...] = h.astype(BF16)

    o_ref[0] = jnp.dot(h_scr[...], w_ref[...], preferred_element_type=F32)


def in_proj(x, g, shift, scale, w):
    b, s, d = x.shape
    n = w.shape[1]
    tm = min(512, s)
    tn = 1152
    return pl.pallas_call(
        _in_proj_kernel,
        grid=(b, s // tm, n // tn),
        in_specs=[
            pl.BlockSpec((1, tm, d), lambda bi, i, j: (bi, i, 0)),
            pl.BlockSpec((1, d), lambda bi, i, j: (0, 0)),
            pl.BlockSpec((1, 1, d), lambda bi, i, j: (bi, 0, 0)),
            pl.BlockSpec((1, 1, d), lambda bi, i, j: (bi, 0, 0)),
            pl.BlockSpec((d, tn), lambda bi, i, j: (0, j)),
        ],
        out_specs=pl.BlockSpec((1, tm, tn), lambda bi, i, j: (bi, i, j)),
        out_shape=jax.ShapeDtypeStruct((b, s, n), F32),
        scratch_shapes=[pltpu.VMEM((tm, d), BF16)],
        compiler_params=_cparams(("parallel", "parallel", "arbitrary")),
        name="in_proj",
    )(x, g.reshape(1, d), shift.reshape(b, 1, d), scale.reshape(b, 1, d), w)


def _qk_prep_kernel(q_ref, k_ref, v_ref, cos_ref, sin_ref, gq_ref, gk_ref, seg_ref, rot_ref,
                    qo_ref, ko_ref, vo_ref):
    cos = cos_ref[0]
    sin = sin_ref[0]
    seg = seg_ref[...]
    rot_m = rot_ref[...]

    def prep(x_ref, g, o_ref, out_scale):
        for c in range(DA_WIDTH // LANES):
            sl = slice(c * LANES, (c + 1) * LANES)
            x = x_ref[0, :, sl]
            ss = jnp.dot((x * x).astype(BF16), seg, preferred_element_type=F32)
            y = x * lax.rsqrt(ss * (1.0 / DA_QK_DIM) + RMS_EPS) * g
            r = jnp.dot(y.astype(BF16), rot_m, preferred_element_type=F32)
            o_ref[0, :, sl] = ((y * cos + r * sin) * out_scale).astype(BF16)

    prep(q_ref, gq_ref[...], qo_ref, DA_QK_DIM ** -0.5)
    prep(k_ref, gk_ref[...], ko_ref, 1.0)
    vo_ref[0] = v_ref[0].astype(BF16)


def qk_prep(proj, cos_t, sin_t, gq, gk):
    b, s, _ = proj.shape
    ts = min(512, s)
    lane = jnp.arange(LANES)
    seg_m = (lane[:, None] // DA_QK_DIM == lane[None, :] // DA_QK_DIM).astype(BF16)
    off = lane % DA_QK_DIM
    half = ROT_DIM // 2
    src = lane[:, None]
    dst = lane[None, :]
    rot_m = (jnp.where((off[None, :] < half) & (src == dst + half), -1.0, 0.0)
             + jnp.where((off[None, :] >= half) & (off[None, :] < ROT_DIM) & (src == dst - half), 1.0, 0.0)
             ).astype(BF16)
    gq_t = jnp.tile(gq, LANES // DA_QK_DIM).reshape(1, LANES)
    gk_t = jnp.tile(gk, LANES // DA_QK_DIM).reshape(1, LANES)
    blk = lambda c: pl.BlockSpec((1, ts, DA_WIDTH), lambda bi, i, c=c: (bi, i, c))
    const = lambda shape: pl.BlockSpec(shape, lambda bi, i: (0,) * len(shape))
    out_sds = jax.ShapeDtypeStruct((b, s, DA_WIDTH), BF16)
    return pl.pallas_call(
        _qk_prep_kernel,
        grid=(b, s // ts),
        in_specs=[blk(0), blk(1), blk(2),
                  pl.BlockSpec((1, ts, LANES), lambda bi, i: (bi, i, 0)),
                  pl.BlockSpec((1, ts, LANES), lambda bi, i: (bi, i, 0)),
                  const((1, LANES)), const((1, LANES)),
                  const((LANES, LANES)), const((LANES, LANES))],
        out_specs=[pl.BlockSpec((1, ts, DA_WIDTH), lambda bi, i: (bi, i, 0))] * 3,
        out_shape=[out_sds] * 3,
        compiler_params=_cparams(("parallel", "parallel")),
        name="qk_prep",
    )(proj, proj, proj, cos_t, sin_t, gq_t, gk_t, seg_m, rot_m)


def _attn_kernel(lam_ref, q_ref, k_ref, v_ref, g_ref, o_ref, *, tq, tk, out_scale):
    qi = pl.program_id(2)
    q = q_ref[0]
    lane = lax.broadcasted_iota(jnp.int32, (tq, LANES), 1)
    zero = jnp.zeros_like(q)
    q0 = jnp.where(lane < DA_QK_DIM, q, zero)
    q1 = jnp.where(lane >= DA_QK_DIM, q, zero)
    row = qi * tq + lax.broadcasted_iota(jnp.int32, (tq, tk), 0)
    col0 = lax.broadcasted_iota(jnp.int32, (tq, tk), 1)
    nt = (((1,), (1,)), ((), ()))

    def update(s, m, l, acc, v):
        m_new = jnp.maximum(m, jnp.max(s, axis=-1, keepdims=True))
        p = jnp.exp(s - m_new)
        alpha = jnp.exp(m - m_new)
        l = alpha * l + jnp.sum(p, axis=-1, keepdims=True)
        acc = alpha * acc + jnp.dot(p.astype(BF16), v, preferred_element_type=F32)
        return m_new, l, acc

    def body(j, carry):
        m0, l0, a0, m1, l1, a1 = carry
        start = pl.multiple_of(j * tk, tk)
        k = k_ref[0, pl.ds(start, tk), :]
        v = v_ref[0, pl.ds(start, tk), :]
        ok = (col0 + j * tk) <= row
        s0 = jnp.where(ok, lax.dot_general(q0, k, nt, preferred_element_type=F32), NEG_BIG)
        s1 = jnp.where(ok, lax.dot_general(q1, k, nt, preferred_element_type=F32), NEG_BIG)
        m0, l0, a0 = update(s0, m0, l0, a0, v)
        m1, l1, a1 = update(s1, m1, l1, a1, v)
        return m0, l0, a0, m1, l1, a1

    mi = jnp.full((tq, 1), NEG_BIG, F32)
    li = jnp.zeros((tq, 1), F32)
    ai = jnp.zeros((tq, DA_V_DIM), F32)
    n_kv = ((qi + 1) * tq + tk - 1) // tk
    m0, l0, a0, m1, l1, a1 = lax.fori_loop(0, n_kv, body, (mi, li, ai, mi, li, ai))
    out = a0 / l0 - lam_ref[0] * (a1 / l1)
    ms = jnp.mean(out * out, axis=-1, keepdims=True)
    o_ref[0] = (out * lax.rsqrt(ms + RMS_EPS) * g_ref[...] * out_scale).astype(BF16)


def diff_attention(qn, kn, vn, lam, subln_g, lam_init):
    b, s, _ = qn.shape
    tq = min(512, s)
    tk = min(256, s)
    kern = functools.partial(_attn_kernel, tq=tq, tk=tk, out_scale=1.0 - lam_init)
    return pl.pallas_call(
        kern,
        grid=(b, DA_HEADS, s // tq),
        in_specs=[
            pl.BlockSpec(memory_space=pltpu.SMEM),
            pl.BlockSpec((1, tq, LANES), lambda bi, h, i: (bi, i, h)),
            pl.BlockSpec((1, s, LANES), lambda bi, h, i: (bi, 0, h)),
            pl.BlockSpec((1, s, LANES), lambda bi, h, i: (bi, 0, h)),
            pl.BlockSpec((1, LANES), lambda bi, h, i: (0, 0)),
        ],
        out_specs=pl.BlockSpec((1, tq, LANES), lambda bi, h, i: (bi, i, h)),
        out_shape=jax.ShapeDtypeStruct((b, s, DA_WIDTH), BF16),
        compiler_params=_cparams(("parallel", "parallel", "parallel")),
        name="diff_attn",
    )(lam.reshape(1).astype(F32), qn, kn, vn, subln_g.reshape(1, LANES))


def _split_bf16(x):
    hi = x.astype(BF16)
    lo = (x - hi.astype(F32)).astype(BF16)
    return hi, lo


def _silu(x):
    return x * jax.nn.sigmoid(x)


def _ssd_kernel(z_ref, xs_ref, bc_ref, dt_ref, cwx_ref, cwb_ref, cbx_ref, cbb_ref, dtb_ref,
                alog_ref, dsk_ref, ng_ref, tri_ref, exp_ref, o_ref, pxs, pbc, st):
    q = SSD_CHUNK

    @pl.when(pl.program_id(1) == 0)
    def _():
        pxs[...] = jnp.zeros_like(pxs)
        pbc[...] = jnp.zeros_like(pbc)
        st[...] = jnp.zeros_like(st)

    def conv(x, prev, w_ref, b_ref):
        acc = x * w_ref[CONV_WIDTH - 1:CONV_WIDTH, :] + b_ref[...]
        row8 = lax.broadcasted_iota(jnp.int32, (SUBLANES, x.shape[1]), 0)
        for k in range(1, CONV_WIDTH):
            xr = pltpu.roll(x, k, 0)
            pr = pltpu.roll(prev, k, 0)
            first = jnp.where(row8 < k, pr, xr[0:SUBLANES])
            xk = jnp.concatenate([first, xr[SUBLANES:]], axis=0)
            acc = acc + xk * w_ref[CONV_WIDTH - 1 - k:CONV_WIDTH - k, :]
        return acc

    xs_raw = xs_ref[0]
    bc_raw = bc_ref[0]
    xs = _silu(conv(xs_raw, pxs[...], cwx_ref, cbx_ref))
    bcs = _silu(conv(bc_raw, pbc[...], cwb_ref, cbb_ref))
    pxs[...] = xs_raw[q - SUBLANES:q]
    pbc[...] = bc_raw[q - SUBLANES:q]

    gw = SSD_GROUPS * SSD_STATE
    bm = [bcs[:, g * SSD_STATE:(g + 1) * SSD_STATE] for g in range(SSD_GROUPS)]
    cm = [bcs[:, gw + g * SSD_STATE:gw + (g + 1) * SSD_STATE].astype(BF16) for g in range(SSD_GROUPS)]

    v = dt_ref[0] + dtb_ref[...]
    dt = jnp.maximum(v, 0.0) + jnp.log1p(jnp.exp(-jnp.abs(v)))
    a_dt = dt * (-jnp.exp(alog_ref[...]))
    tri = tri_ref[...]
    a_hi, a_lo = _split_bf16(a_dt)
    a_cs = (jnp.dot(tri, a_hi, preferred_element_type=F32)
            + jnp.dot(tri, a_lo, preferred_element_type=F32))
    a_cs_t = a_cs.T
    exp_a = jnp.exp(a_cs)
    decay = jnp.exp(a_cs[q - 1:q, :] - a_cs)
    stack = jnp.concatenate([dt, exp_a, decay], axis=0)
    s_hi, s_lo = _split_bf16(stack)
    ex = (jnp.dot(s_hi, exp_ref[...], preferred_element_type=F32)
          + jnp.dot(s_lo, exp_ref[...], preferred_element_type=F32))
    dt_e = ex[0:q]
    exp_a_e = ex[q:2 * q]
    decay_e = ex[2 * q:3 * q]
    x_dt = xs * dt_e
    x_dec = x_dt * decay_e

    nt = (((1,), (1,)), ((), ()))
    cb = [lax.dot_general(cm[g], bm[g].astype(BF16), nt, preferred_element_type=F32)
          for g in range(SSD_GROUPS)]
    b_t = [bm[g].T.astype(BF16) for g in range(SSD_GROUPS)]
    causal = (lax.broadcasted_iota(jnp.int32, (q, q), 0) >= lax.broadcasted_iota(jnp.int32, (q, q), 1))
    lane = lax.broadcasted_iota(jnp.int32, (q, LANES), 1)
    first_head = lane < SSD_HEAD_DIM
    z = z_ref[0]
    heads_per_group = SSD_HEADS // SSD_GROUPS
    pairs = []
    for j in range(SSD_HEADS // 2):
        sl = slice(j * LANES, (j + 1) * LANES)
        g = (2 * j) // heads_per_group
        xp = x_dt[:, sl]
        y = jnp.zeros((q, LANES), F32)
        for hh in range(2):
            h = 2 * j + hh
            seg = a_cs[:, h:h + 1] - a_cs_t[h:h + 1, :]
            decay_l = jnp.exp(jnp.where(causal, seg, -jnp.inf))
            gm = (cb[g] * decay_l).astype(BF16)
            keep = first_head if hh == 0 else jnp.logical_not(first_head)
            xm = jnp.where(keep, xp, 0.0).astype(BF16)
            y = y + jnp.dot(gm, xm, preferred_element_type=F32)
        sp = st[j]
        y_off = jnp.dot(cm[g], sp.astype(BF16), preferred_element_type=F32) * exp_a_e[:, sl]
        st[j] = (sp * exp_a_e[q - 1:q, sl]
                 + jnp.dot(b_t[g], x_dec[:, sl].astype(BF16), preferred_element_type=F32))
        yp = y + y_off + dsk_ref[:, sl] * xs[:, sl]
        pairs.append(yp * _silu(z[:, sl]))
    y = jnp.concatenate(pairs, axis=1)
    gwid = SSD_INNER // SSD_GROUPS
    outs = []
    for g in range(SSD_GROUPS):
        yg = y[:, g * gwid:(g + 1) * gwid]
        ms = jnp.mean(yg * yg, axis=-1, keepdims=True)
        outs.append(yg * lax.rsqrt(ms + RMS_EPS) * ng_ref[:, g * gwid:(g + 1) * gwid])
    o_ref[0] = jnp.concatenate(outs, axis=1).astype(BF16)


def ssd_group(proj, conv_w, conv_b, dt_bias, a_log, d_skip, norm_g):
    b, s, _ = proj.shape
    q = SSD_CHUNK
    pad = lambda vec: jnp.pad(vec, (0, LANES - SSD_HEADS)).reshape(1, LANES)
    idx = jnp.arange(q)
    tri = (idx[:, None] >= idx[None, :]).astype(BF16)
    expand = (jnp.arange(LANES)[:, None] == (jnp.arange(SSD_INNER)[None, :] // SSD_HEAD_DIM)).astype(BF16)
    bcw = 2 * SSD_GROUPS * SSD_STATE
    const = lambda shape: pl.BlockSpec(shape, lambda bi, c: (0,) * len(shape))
    return pl.pallas_call(
        _ssd_kernel,
        grid=(b, s // q),
        in_specs=[
            pl.BlockSpec((1, q, SSD_INNER), lambda bi, c: (bi, c, _Z_BLK)),
            pl.BlockSpec((1, q, SSD_INNER), lambda bi, c: (bi, c, _XS_BLK)),
            pl.BlockSpec((1, q, bcw), lambda bi, c: (bi, c, _BC_BLK)),
            pl.BlockSpec((1, q, LANES), lambda bi, c: (bi, c, _DT_BLK)),
            const((CONV_WIDTH, SSD_INNER)), const((CONV_WIDTH, bcw)),
            const((1, SSD_INNER)), const((1, bcw)),
            const((1, LANES)), const((1, LANES)),
            const((1, SSD_INNER)), const((1, SSD_INNER)),
            const((q, q)), const((LANES, SSD_INNER)),
        ],
        out_specs=pl.BlockSpec((1, q, SSD_INNER), lambda bi, c: (bi, c, 0)),
        out_shape=jax.ShapeDtypeStruct((b, s, SSD_INNER), BF16),
        scratch_shapes=[pltpu.VMEM((SUBLANES, SSD_INNER), F32),
                        pltpu.VMEM((SUBLANES, bcw), F32),
                        pltpu.VMEM((SSD_HEADS // 2, SSD_STATE, LANES), F32)],
        compiler_params=_cparams(("parallel", "arbitrary")),
        name="ssd",
    )(proj, proj, proj, proj,
      conv_w[:, :SSD_INNER], conv_w[:, SSD_INNER:],
      conv_b[:SSD_INNER].reshape(1, -1), conv_b[SSD_INNER:].reshape(1, -1),
      pad(dt_bias), pad(a_log),
      jnp.repeat(d_skip, SSD_HEAD_DIM).reshape(1, -1), norm_g.reshape(1, -1),
      tri, expand)


def _out_proj_kernel(a_ref, s_ref, w_ref, x_ref, g_ref, o_ref):
    acc = jnp.dot(a_ref[0], w_ref[0:DA_WIDTH, :], preferred_element_type=F32)
    acc = acc + jnp.dot(s_ref[0], w_ref[DA_WIDTH:, :], preferred_element_type=F32)
    o_ref[0] = x_ref[0] + g_ref[0] * acc


def out_proj(attn, ssd, w, x, gate):
    b, s, d = x.shape
    tm = min(512, s)
    return pl.pallas_call(
        _out_proj_kernel,
        grid=(b, s // tm),
        in_specs=[
            pl.BlockSpec((1, tm, DA_WIDTH), lambda bi, i: (bi, i, 0)),
            pl.BlockSpec((1, tm, SSD_INNER), lambda bi, i: (bi, i, 0)),
            pl.BlockSpec((d, d), lambda bi, i: (0, 0)),
            pl.BlockSpec((1, tm, d), lambda bi, i: (bi, i, 0)),
            pl.BlockSpec((1, 1, d), lambda bi, i: (bi, 0, 0)),
        ],
        out_specs=pl.BlockSpec((1, tm, d), lambda bi, i: (bi, i, 0)),
        out_shape=jax.ShapeDtypeStruct((b, s, d), F32),
        compiler_params=_cparams(("parallel", "parallel")),
        name="out_proj",
    )(attn, ssd, w, x, gate.reshape(b, 1, d))


def _extract_topk(vals, pos, payload, k):
    n = vals.shape[1]
    rowk = lax.broadcasted_iota(jnp.int32, (k, n), 0)
    top_v = jnp.zeros((k, n), F32)
    top_p = jnp.zeros((k, n), F32)
    big = jnp.float32(1e9)
    for r in range(k):
        m = jnp.max(vals, axis=0, keepdims=True)
        p = jnp.min(jnp.where(vals == m, pos, big), axis=0, keepdims=True)
        sel = pos == p
        if payload is None:
            pay = p
        else:
            pay = jnp.max(jnp.where(sel, payload, -1.0), axis=0, keepdims=True)
        top_v = jnp.where(rowk == r, m, top_v)
        top_p = jnp.where(rowk == r, pay, top_p)
        vals = jnp.where(sel, -jnp.inf, vals)
    return top_v, top_p


def _route_kernel(x_ref, g_ref, sh_ref, sc_ref, wq_ref, k1_ref, k2_ref, h_ref, idx_ref, gate_ref):
    tm = x_ref.shape[1]
    h = _norm_modulate(x_ref[0], g_ref[...], sh_ref[0], sc_ref[0])
    h_ref[0] = h
    qv = jnp.dot(h.astype(BF16), wq_ref[...], preferred_element_type=F32)
    nt = (((1,), (1,)), ((), ()))
    key_pos = lax.broadcasted_iota(jnp.int32, (PEER_N_KEYS, tm), 0).astype(F32)
    k = PEER_TOPK
    sub = lax.broadcasted_iota(jnp.int32, (SUBLANES, tm), 0).astype(F32)
    sub16 = lax.broadcasted_iota(jnp.int32, (k, tm), 0).astype(F32)
    idx_rows = []
    gate_rows = []
    for hd in range(PEER_HEADS):
        base = hd * 2 * PEER_HALF
        qa = qv[:, base:base + PEER_HALF].astype(BF16)
        qb = qv[:, base + PEER_HALF:base + 2 * PEER_HALF].astype(BF16)
        s1 = lax.dot_general(k1_ref[hd], qa, nt, preferred_element_type=F32)
        s2 = lax.dot_general(k2_ref[hd], qb, nt, preferred_element_type=F32)
        v1, i1 = _extract_topk(s1, key_pos, None, k)
        v2, i2 = _extract_topk(s2, key_pos, None, k)
        cv = [v1[0:1] + v2]
        ci = [i1[0:1] * PEER_N_KEYS + i2]
        cp = [sub16]
        for a in range(1, SUBLANES):
            cv.append(v1[a:a + 1] + v2[0:SUBLANES])
            ci.append(i1[a:a + 1] * PEER_N_KEYS + i2[0:SUBLANES])
            cp.append(sub + float(a * k))
        cv.append(v1[SUBLANES:k] + v2[0:1])
        ci.append(i1[SUBLANES:k] * PEER_N_KEYS + i2[0:1])
        cp.append((sub + float(SUBLANES)) * float(k))
        cand_v = jnp.concatenate(cv, axis=0)
        cand_i = jnp.concatenate(ci, axis=0)
        cand_p = jnp.concatenate(cp, axis=0)
        top, eidx = _extract_topk(cand_v, cand_p, cand_i, k)
        e = jnp.exp(top - jnp.max(top, axis=0, keepdims=True))
        gate_rows.append(e / jnp.sum(e, axis=0, keepdims=True))
        idx_rows.append(eidx)
    idx_ref[0] = jnp.concatenate(idx_rows, axis=0).T.astype(jnp.int32)
    gate_ref[0] = jnp.concatenate(gate_rows, axis=0).T


def peer_route(x, g, shift, scale, wq, k1, k2):
    b, s, d = x.shape
    tm = min(256, s)
    nsel = PEER_HEADS * PEER_TOPK
    const = lambda shape: pl.BlockSpec(shape, lambda bi, i: (0,) * len(shape))
    return pl.pallas_call(
        _route_kernel,
        grid=(b, s // tm),
        in_specs=[
            pl.BlockSpec((1, tm, d), lambda bi, i: (bi, i, 0)),
            const((1, d)),
            pl.BlockSpec((1, 1, d), lambda bi, i: (bi, 0, 0)),
            pl.BlockSpec((1, 1, d), lambda bi, i: (bi, 0, 0)),
            const((d, d)),
            const((PEER_HEADS, PEER_N_KEYS, PEER_HALF)),
            const((PEER_HEADS, PEER_N_KEYS, PEER_HALF)),
        ],
        out_specs=[pl.BlockSpec((1, tm, d), lambda bi, i: (bi, i, 0)),
                   pl.BlockSpec((1, tm, nsel), lambda bi, i: (bi, i, 0)),
                   pl.BlockSpec((1, tm, nsel), lambda bi, i: (bi, i, 0))],
        out_shape=[jax.ShapeDtypeStruct((b, s, d), F32),
                   jax.ShapeDtypeStruct((b, s, nsel), jnp.int32),
                   jax.ShapeDtypeStruct((b, s, nsel), F32)],
        compiler_params=_cparams(("parallel", "parallel")),
        name="peer_route",
    )(x, g.reshape(1, d), shift.reshape(b, 1, d), scale.reshape(b, 1, d), wq, k1, k2)


_ROW_TILES = D_MODEL // LANES
_NSEL = PEER_HEADS * PEER_TOPK
_GROUP = 8
_GROUPS_IN_FLIGHT = 3
_GROUPS_AHEAD = _GROUPS_IN_FLIGHT - 1
_PACK = 2 * SUBLANES
_BURST = 8
_VSUM = 4


def _expert_kernel(idx_ref, head_ref, gate_ref, h_ref, sel_ref, uv_hbm, o_ref, buf, sem, pscr, splat):
    te = h_ref.shape[0]
    n_groups = te // _GROUP
    ones = jnp.ones((_PACK, LANES), BF16)
    nt = (((1,), (1,)), ((), ()))

    def row_copies(index_of, g, slot):
        starts = []
        for i in range(_GROUP):
            t = g * _GROUP + i
            for k in range(_NSEL):
                starts.append(functools.partial(
                    lambda t, i, k: pltpu.make_async_copy(
                        uv_hbm.at[index_of(t, k)], buf.at[slot, i, k], sem.at[slot]).start(priority=k % 2),
                    t, i, k))
        return starts

    this_step = lambda t, k: idx_ref[t, k]
    next_step = lambda t, k: head_ref[0, t, k]

    def wait(slot):
        for i in range(_GROUP):
            pltpu.make_async_copy(uv_hbm.at[pl.ds(0, _NSEL)], buf.at[slot, i], sem.at[slot]).wait()

    def start_next(pending, n):
        for _ in range(n):
            start = next(pending, None)
            if start is not None:
                start()

    def u_products(g, slot, i, pending):
        t = g * _GROUP + i
        hb = h_ref[t].astype(BF16)
        for kk in range(_NSEL // 2):
            parts = []
            for k in (2 * kk, 2 * kk + 1):
                pr = (buf[slot, i, k, 0:_ROW_TILES, :] * hb).astype(F32)
                parts.append(pr[0:SUBLANES] + pr[SUBLANES:_ROW_TILES])
            pscr[i, kk * _PACK:(kk + 1) * _PACK, :] = jnp.concatenate(parts, axis=0).astype(BF16)
            if kk % _BURST == _BURST - 1:
                start_next(pending, _BURST)
        return jnp.dot(sel_ref[...], pscr[i], preferred_element_type=F32)

    def u_finish(g, i, m):
        t = g * _GROUP + i
        m_hi, m_lo = _split_bf16(m)
        a = (lax.dot_general(ones, m_hi, nt, preferred_element_type=F32)
             + lax.dot_general(ones, m_lo, nt, preferred_element_type=F32))[0:1]
        act = 0.5 * a * (1.0 + lax.erf(a * (2.0 ** -0.5))) * gate_ref[pl.ds(t, 1), :]
        hi = lax.shift_right_logical(pltpu.bitcast(act.astype(BF16).astype(F32), jnp.uint32), jnp.uint32(16))
        return pltpu.bitcast(lax.shift_left(hi, jnp.uint32(16)) | hi, F32)

    def v_token(g, slot, i, pending):
        t = g * _GROUP + i
        accs = [jnp.zeros((_ROW_TILES, LANES), F32) for _ in range(2)]
        for kq in range(_NSEL // _VSUM):
            part = None
            for k in range(kq * _VSUM, (kq + 1) * _VSUM):
                w = pltpu.bitcast(jnp.broadcast_to(splat[i, pl.ds(k, 1), :], (SUBLANES, LANES)), BF16)
                term = w * buf[slot, i, k, _ROW_TILES:2 * _ROW_TILES, :]
                part = term if part is None else part + term
            accs[kq % 2] = accs[kq % 2] + part.astype(F32)
            if kq % (2 * _BURST // _VSUM) == 2 * _BURST // _VSUM - 1:
                start_next(pending, _BURST)
        o_ref[t] = accs[0] + accs[1]

    def step(g_u, slot_u, g_v, slot_v, words, pending=iter(())):
        if words is not None:
            for i in range(_GROUP):
                splat[i] = jnp.broadcast_to(words[i], (LANES, LANES)).T
        nxt = []
        for i in range(_GROUP):
            m = u_products(g_u, slot_u, i, pending) if g_u is not None else None
            if words is not None:
                v_token(g_v, slot_v, i, pending)
            if m is not None:
                nxt.append(u_finish(g_u, i, m))
        return tuple(nxt)

    nif = _GROUPS_IN_FLIGHT
    ahead = _GROUPS_AHEAD
    grid_step = pl.program_id(0)
    slot_of = lambda g: lax.rem(grid_step * n_groups + g, nif)

    @pl.when(grid_step == 0)
    def _():
        for g0 in range(ahead):
            for start in row_copies(this_step, g0, g0):
                start()

    wait(slot_of(0))
    words0 = step(0, slot_of(0), None, None, None)

    def pipeline_step(index_of, g_issue, g, words):
        pending = iter(row_copies(index_of, g_issue, slot_of(g + ahead)))
        wait(slot_of(g + 1))
        nxt = step(g + 1, slot_of(g + 1), g, slot_of(g), words, pending)
        start_next(pending, _GROUP * _NSEL)
        return nxt

    steady = lambda g, words: pipeline_step(this_step, g + ahead, g, words)
    boundary = lambda g, words: pipeline_step(next_step, g + ahead - n_groups, g, words)

    zero = 0 * grid_step
    n_steady = n_groups - ahead
    words = lax.fori_loop(zero, zero + n_steady, steady, words0)
    words = lax.fori_loop(zero + n_steady, zero + (n_groups - 1), boundary, words)
    last = n_groups - 1
    pending = iter(row_copies(next_step, ahead - 1, slot_of(last + ahead)))
    step(None, None, last, slot_of(last), words, pending)
    start_next(pending, _GROUP * _NSEL)

    @pl.when(grid_step == pl.num_programs(0) - 1)
    def _():
        for j in range(ahead):
            wait(slot_of(n_groups + j))


def peer_experts(idx, gate, h3, uv):
    m = idx.shape[0]
    te = min(256, m)
    sel = (jnp.arange(_NSEL)[:, None] == jnp.arange(_NSEL * SUBLANES)[None, :] // SUBLANES).astype(BF16)
    n_steps = m // te
    n_head = _GROUPS_AHEAD * _GROUP
    head = idx.reshape(n_steps, te, _NSEL)[:, :n_head]
    return pl.pallas_call(
        _expert_kernel,
        grid=(n_steps,),
        in_specs=[
            pl.BlockSpec((te, _NSEL), lambda i: (i, 0), memory_space=pltpu.SMEM),
            pl.BlockSpec((1, n_head, _NSEL), lambda i: (jnp.minimum(i + 1, n_steps - 1), 0, 0),
                         memory_space=pltpu.SMEM),
            pl.BlockSpec((te, _NSEL), lambda i: (i, 0)),
            pl.BlockSpec((te, _ROW_TILES, LANES), lambda i: (i, 0, 0)),
            pl.BlockSpec((_NSEL, _NSEL * SUBLANES), lambda i: (0, 0)),
            pl.BlockSpec(memory_space=pl.ANY),
        ],
        out_specs=pl.BlockSpec((te, _ROW_TILES, LANES), lambda i: (i, 0, 0)),
        out_shape=jax.ShapeDtypeStruct((m, _ROW_TILES, LANES), F32),
        scratch_shapes=[
            pltpu.VMEM((_GROUPS_IN_FLIGHT, _GROUP, _NSEL, 2 * _ROW_TILES, LANES), BF16),
            pltpu.SemaphoreType.DMA((_GROUPS_IN_FLIGHT,)),
            pltpu.VMEM((_GROUP, _NSEL * SUBLANES, LANES), BF16),
            pltpu.VMEM((_GROUP, LANES, LANES), F32),
        ],
        compiler_params=_cparams(("arbitrary",)),
        name="peer_experts",
    )(idx, head, gate, h3, sel, uv)


def _residual_kernel(x_ref, y_ref, g_ref, o_ref):
    o_ref[0] = x_ref[0] + g_ref[0] * y_ref[0]


def gated_residual(x, y, gate):
    b, s, d = x.shape
    tm = min(512, s)
    blk = pl.BlockSpec((1, tm, d), lambda bi, i: (bi, i, 0))
    return pl.pallas_call(
        _residual_kernel,
        grid=(b, s // tm),
        in_specs=[blk, blk, pl.BlockSpec((1, 1, d), lambda bi, i: (bi, 0, 0))],
        out_specs=blk,
        out_shape=jax.ShapeDtypeStruct((b, s, d), F32),
        compiler_params=_cparams(("parallel", "parallel")),
        name="gated_residual",
    )(x, y, gate.reshape(b, 1, d))


def _rope_tables(positions):
    inv_freq = ROPE_THETA ** (-jnp.arange(0, ROT_DIM, 2, dtype=F32) / ROT_DIM)
    ang = positions.astype(F32)[..., None] * inv_freq
    cos = jnp.cos(ang)
    sin = jnp.sin(ang)
    b, s, _ = ang.shape
    rest = DA_QK_DIM - ROT_DIM
    cos64 = jnp.concatenate([cos, cos, jnp.ones((b, s, rest), F32)], axis=-1)
    sin64 = jnp.concatenate([sin, sin, jnp.zeros((b, s, rest), F32)], axis=-1)
    return jnp.tile(cos64, (1, 1, LANES // DA_QK_DIM)), jnp.tile(sin64, (1, 1, LANES // DA_QK_DIM))


def kernel(x, c, positions, norm_mix_g, norm_ffn_g, w_ada, b_ada, w_in, q_norm_g, k_norm_g, lam_q1, lam_k1, lam_q2, lam_k2, subln_g, conv_w, conv_b, dt_bias, a_log, d_skip, ssd_norm_g, w_out, peer_wq, peer_k1, peer_k2, peer_u, peer_v):
    depth = w_ada.shape[0]
    b, s, d = x.shape
    cos_t, sin_t = _rope_tables(positions)
    mod = adaln(c, w_ada, b_ada)
    for i in range(depth):
        lam_init = 0.8 - 0.6 * math.exp(-0.3 * i)
        sh_m, sc_m, g_m, sh_f, sc_f, g_f = [mod[i, :, j * d:(j + 1) * d] for j in range(6)]
        w_in_p = jnp.pad(w_in[i], ((0, 0), (0, IN_DIM_PAD - IN_DIM))).astype(BF16)
        proj = in_proj(x, norm_mix_g[i], sh_m, sc_m, w_in_p)
        qn, kn, vn = qk_prep(proj, cos_t, sin_t, q_norm_g[i], k_norm_g[i])
        lam = (jnp.exp(jnp.sum(lam_q1[i] * lam_k1[i])) - jnp.exp(jnp.sum(lam_q2[i] * lam_k2[i])) + lam_init)
        attn = diff_attention(qn, kn, vn, lam, subln_g[i], lam_init)
        ssd = ssd_group(proj, conv_w[i], conv_b[i], dt_bias[i], a_log[i], d_skip[i], ssd_norm_g[i])
        x = out_proj(attn, ssd, w_out[i].astype(BF16), x, g_m)
        h2, idx, gate = peer_route(x, norm_ffn_g[i], sh_f, sc_f, peer_wq[i].astype(BF16),
                                   peer_k1[i].astype(BF16), peer_k2[i].astype(BF16))
        n_exp = peer_u.shape[1]
        uv = jnp.concatenate([peer_u[i].astype(BF16).reshape(n_exp, _ROW_TILES, LANES),
                              peer_v[i].astype(BF16).reshape(n_exp, _ROW_TILES, LANES)], axis=1)
        y3 = peer_experts(idx.reshape(b * s, _NSEL), gate.reshape(b * s, _NSEL),
                          h2.reshape(b * s, _ROW_TILES, LANES), uv)
        x = gated_residual(x, y3.reshape(b, s, d), g_f)
    return x
```

```python
import functools
import math

import jax
import jax.numpy as jnp
from jax import lax
from jax.experimental import pallas as pl
from jax.experimental.pallas import tpu as pltpu

F32 = jnp.float32
BF16 = jnp.bfloat16

D_MODEL = 2048
DA_WIDTH = 1024
DA_HEADS = 8
DA_V_DIM = 128
DA_QK_DIM = 64
ROT_DIM = 16
ROPE_THETA = 500000.0
SSD_INNER = 1024
SSD_HEADS = 16
SSD_HEAD_DIM = 64
SSD_GROUPS = 2
SSD_STATE = 128
SSD_CHUNK = 128
CONV_WIDTH = 4
IN_DIM = 5648
PEER_HEADS = 8
PEER_N_KEYS = 128
PEER_HALF = 128
PEER_TOPK = 16
RMS_EPS = 1e-6

LANES = 128
SUBLANES = 8
IN_DIM_PAD = 5760
VMEM_LIMIT = 52 * 1024 * 1024

_Z_BLK = 3
_XS_BLK = 4
_BC_BLK = 10
_DT_BLK = 44

NEG_BIG = -1e30


def _cparams(sem):
    return pltpu.CompilerParams(dimension_semantics=sem, vmem_limit_bytes=VMEM_LIMIT)


def _adaln_kernel(c_ref, w_ref, b_ref, o_ref):
    c = c_ref[...]
    cond = (c * jax.nn.sigmoid(c)).astype(BF16)
    w = w_ref[0].astype(BF16)
    o_ref[0] = jnp.dot(cond, w, preferred_element_type=F32) + b_ref[0]


def adaln(c, w_ada, b_ada):
    depth, d, n = w_ada.shape
    b = c.shape[0]
    tn = 1024
    return pl.pallas_call(
        _adaln_kernel,
        grid=(depth, n // tn),
        in_specs=[
            pl.BlockSpec((b, d), lambda l, j: (0, 0)),
            pl.BlockSpec((1, d, tn), lambda l, j: (l, 0, j)),
            pl.BlockSpec((1, 1, tn), lambda l, j: (l, 0, j)),
        ],
        out_specs=pl.BlockSpec((1, b, tn), lambda l, j: (l, 0, j)),
        out_shape=jax.ShapeDtypeStruct((depth, b, n), F32),
        compiler_params=_cparams(("parallel", "parallel")),
        name="adaln",
    )(c, w_ada, b_ada.reshape(depth, 1, n))


def _norm_modulate(x, g, shift, scale):
    ms = jnp.mean(x * x, axis=-1, keepdims=True)
    y = x * lax.rsqrt(ms + RMS_EPS) * g
    return y * (1.0 + scale) + shift


def _in_proj_kernel(x_ref, g_ref, sh_ref, sc_ref, w_ref, o_ref, h_scr):
    @pl.when(pl.program_id(2) == 0)
    def _():
        h = _norm_modulate(x_ref[0], g_ref[...], sh_ref[0], sc_ref[0])
        h_scr[...] = h.astype(BF16)

    o_ref[0] = jnp.dot(h_scr[...], w_ref[...], preferred_element_type=F32)


def in_proj(x, g, shift, scale, w):
    b, s, d = x.shape
    n = w.shape[1]
    tm = min(512, s)
    tn = 1152
    return pl.pallas_call(
        _in_proj_kernel,
        grid=(b, s // tm, n // tn),
        in_specs=[
            pl.BlockSpec((1, tm, d), lambda bi, i, j: (bi, i, 0)),
            pl.BlockSpec((1, d), lambda bi, i, j: (0, 0)),
            pl.BlockSpec((1, 1, d), lambda bi, i, j: (bi, 0, 0)),
            pl.BlockSpec((1, 1, d), lambda bi, i, j: (bi, 0, 0)),
            pl.BlockSpec((d, tn), lambda bi, i, j: (0, j)),
        ],
        out_specs=pl.BlockSpec((1, tm, tn), lambda bi, i, j: (bi, i, j)),
        out_shape=jax.ShapeDtypeStruct((b, s, n), F32),
        scratch_shapes=[pltpu.VMEM((tm, d), BF16)],
        compiler_params=_cparams(("parallel", "parallel", "arbitrary")),
        name="in_proj",
    )(x, g.reshape(1, d), shift.reshape(b, 1, d), scale.reshape(b, 1, d), w)


def _qk_prep_kernel(q_ref, k_ref, v_ref, cos_ref, sin_ref, gq_ref, gk_ref, seg_ref, rot_ref,
                    qo_ref, ko_ref, vo_ref):
    cos = cos_ref[0]
    sin = sin_ref[0]
    seg = seg_ref[...]
    rot_m = rot_ref[...]

    def prep(x_ref, g, o_ref, out_scale):
        for c in range(DA_WIDTH // LANES):
            sl = slice(c * LANES, (c + 1) * LANES)
            x = x_ref[0, :, sl]
            ss = jnp.dot((x * x).astype(BF16), seg, preferred_element_type=F32)
            y = x * lax.rsqrt(ss * (1.0 / DA_QK_DIM) + RMS_EPS) * g
            r = jnp.dot(y.astype(BF16), rot_m, preferred_element_type=F32)
            o_ref[0, :, sl] = ((y * cos + r * sin) * out_scale).astype(BF16)

    prep(q_ref, gq_ref[...], qo_ref, DA_QK_DIM ** -0.5)
    prep(k_ref, gk_ref[...], ko_ref, 1.0)
    vo_ref[0] = v_ref[0].astype(BF16)


def qk_prep(proj, cos_t, sin_t, gq, gk):
    b, s, _ = proj.shape
    ts = min(512, s)
    lane = jnp.arange(LANES)
    seg_m = (lane[:, None] // DA_QK_DIM == lane[None, :] // DA_QK_DIM).astype(BF16)
    off = lane % DA_QK_DIM
    half = ROT_DIM // 2
    src = lane[:, None]
    dst = lane[None, :]
    rot_m = (jnp.where((off[None, :] < half) & (src == dst + half), -1.0, 0.0)
             + jnp.where((off[None, :] >= half) & (off[None, :] < ROT_DIM) & (src == dst - half), 1.0, 0.0)
             ).astype(BF16)
    gq_t = jnp.tile(gq, LANES // DA_QK_DIM).reshape(1, LANES)
    gk_t = jnp.tile(gk, LANES // DA_QK_DIM).reshape(1, LANES)
    blk = lambda c: pl.BlockSpec((1, ts, DA_WIDTH), lambda bi, i, c=c: (bi, i, c))
    const = lambda shape: pl.BlockSpec(shape, lambda bi, i: (0,) * len(shape))
    out_sds = jax.ShapeDtypeStruct((b, s, DA_WIDTH), BF16)
    return pl.pallas_call(
        _qk_prep_kernel,
        grid=(b, s // ts),
        in_specs=[blk(0), blk(1), blk(2),
                  pl.BlockSpec((1, ts, LANES), lambda bi, i: (bi, i, 0)),
                  pl.BlockSpec((1, ts, LANES), lambda bi, i: (bi, i, 0)),
                  const((1, LANES)), const((1, LANES)),
                  const((LANES, LANES)), const((LANES, LANES))],
        out_specs=[pl.BlockSpec((1, ts, DA_WIDTH), lambda bi, i: (bi, i, 0))] * 3,
        out_shape=[out_sds] * 3,
        compiler_params=_cparams(("parallel", "parallel")),
        name="qk_prep",
    )(proj, proj, proj, cos_t, sin_t, gq_t, gk_t, seg_m, rot_m)


def _attn_kernel(lam_ref, q_ref, k_ref, v_ref, g_ref, o_ref, *, tq, tk, out_scale):
    qi = pl.program_id(2)
    q = q_ref[0]
    lane = lax.broadcasted_iota(jnp.int32, (tq, LANES), 1)
    zero = jnp.zeros_like(q)
    q0 = jnp.where(lane < DA_QK_DIM, q, zero)
    q1 = jnp.where(lane >= DA_QK_DIM, q, zero)
    row = qi * tq + lax.broadcasted_iota(jnp.int32, (tq, tk), 0)
    col0 = lax.broadcasted_iota(jnp.int32, (tq, tk), 1)
    nt = (((1,), (1,)), ((), ()))

    def update(s, m, l, acc, v):
        m_new = jnp.maximum(m, jnp.max(s, axis=-1, keepdims=True))
        p = jnp.exp(s - m_new)
        alpha = jnp.exp(m - m_new)
        l = alpha * l + jnp.sum(p, axis=-1, keepdims=True)
        acc = alpha * acc + jnp.dot(p.astype(BF16), v, preferred_element_type=F32)
        return m_new, l, acc

    def body(j, carry):
        m0, l0, a0, m1, l1, a1 = carry
        start = pl.multiple_of(j * tk, tk)
        k = k_ref[0, pl.ds(start, tk), :]
        v = v_ref[0, pl.ds(start, tk), :]
        ok = (col0 + j * tk) <= row
        s0 = jnp.where(ok, lax.dot_general(q0, k, nt, preferred_element_type=F32), NEG_BIG)
        s1 = jnp.where(ok, lax.dot_general(q1, k, nt, preferred_element_type=F32), NEG_BIG)
        m0, l0, a0 = update(s0, m0, l0, a0, v)
        m1, l1, a1 = update(s1, m1, l1, a1, v)
        return m0, l0, a0, m1, l1, a1

    mi = jnp.full((tq, 1), NEG_BIG, F32)
    li = jnp.zeros((tq, 1), F32)
    ai = jnp.zeros((tq, DA_V_DIM), F32)
    n_kv = ((qi + 1) * tq + tk - 1) // tk
    m0, l0, a0, m1, l1, a1 = lax.fori_loop(0, n_kv, body, (mi, li, ai, mi, li, ai))
    out = a0 / l0 - lam_ref[0] * (a1 / l1)
    ms = jnp.mean(out * out, axis=-1, keepdims=True)
    o_ref[0] = (out * lax.rsqrt(ms + RMS_EPS) * g_ref[...] * out_scale).astype(BF16)


def diff_attention(qn, kn, vn, lam, subln_g, lam_init):
    b, s, _ = qn.shape
    tq = min(512, s)
    tk = min(256, s)
    kern = functools.partial(_attn_kernel, tq=tq, tk=tk, out_scale=1.0 - lam_init)
    return pl.pallas_call(
        kern,
        grid=(b, DA_HEADS, s // tq),
        in_specs=[
            pl.BlockSpec(memory_space=pltpu.SMEM),
            pl.BlockSpec((1, tq, LANES), lambda bi, h, i: (bi, i, h)),
            pl.BlockSpec((1, s, LANES), lambda bi, h, i: (bi, 0, h)),
            pl.BlockSpec((1, s, LANES), lambda bi, h, i: (bi, 0, h)),
            pl.BlockSpec((1, LANES), lambda bi, h, i: (0, 0)),
        ],
        out_specs=pl.BlockSpec((1, tq, LANES), lambda bi, h, i: (bi, i, h)),
        out_shape=jax.ShapeDtypeStruct((b, s, DA_WIDTH), BF16),
        compiler_params=_cparams(("parallel", "parallel", "parallel")),
        name="diff_attn",
    )(lam.reshape(1).astype(F32), qn, kn, vn, subln_g.reshape(1, LANES))


def _split_bf16(x):
    hi = x.astype(BF16)
    lo = (x - hi.astype(F32)).astype(BF16)
    return hi, lo


def _silu(x):
    return x * jax.nn.sigmoid(x)


def _ssd_kernel(z_ref, xs_ref, bc_ref, dt_ref, cwx_ref, cwb_ref, cbx_ref, cbb_ref, dtb_ref,
                alog_ref, dsk_ref, ng_ref, tri_ref, exp_ref, o_ref, pxs, pbc, st):
    q = SSD_CHUNK

    @pl.when(pl.program_id(1) == 0)
    def _():
        pxs[...] = jnp.zeros_like(pxs)
        pbc[...] = jnp.zeros_like(pbc)
        st[...] = jnp.zeros_like(st)

    def conv(x, prev, w_ref, b_ref):
        acc = x * w_ref[CONV_WIDTH - 1:CONV_WIDTH, :] + b_ref[...]
        row8 = lax.broadcasted_iota(jnp.int32, (SUBLANES, x.shape[1]), 0)
        for k in range(1, CONV_WIDTH):
            xr = pltpu.roll(x, k, 0)
            pr = pltpu.roll(prev, k, 0)
            first = jnp.where(row8 < k, pr, xr[0:SUBLANES])
            xk = jnp.concatenate([first, xr[SUBLANES:]], axis=0)
            acc = acc + xk * w_ref[CONV_WIDTH - 1 - k:CONV_WIDTH - k, :]
        return acc

    xs_raw = xs_ref[0]
    bc_raw = bc_ref[0]
    xs = _silu(conv(xs_raw, pxs[...], cwx_ref, cbx_ref))
    bcs = _silu(conv(bc_raw, pbc[...], cwb_ref, cbb_ref))
    pxs[...] = xs_raw[q - SUBLANES:q]
    pbc[...] = bc_raw[q - SUBLANES:q]

    gw = SSD_GROUPS * SSD_STATE
    bm = [bcs[:, g * SSD_STATE:(g + 1) * SSD_STATE] for g in range(SSD_GROUPS)]
    cm = [bcs[:, gw + g * SSD_STATE:gw + (g + 1) * SSD_STATE].astype(BF16) for g in range(SSD_GROUPS)]

    v = dt_ref[0] + dtb_ref[...]
    dt = jnp.maximum(v, 0.0) + jnp.log1p(jnp.exp(-jnp.abs(v)))
    a_dt = dt * (-jnp.exp(alog_ref[...]))
    tri = tri_ref[...]
    a_hi, a_lo = _split_bf16(a_dt)
    a_cs = (jnp.dot(tri, a_hi, preferred_element_type=F32)
            + jnp.dot(tri, a_lo, preferred_element_type=F32))
    a_cs_t = a_cs.T
    exp_a = jnp.exp(a_cs)
    decay = jnp.exp(a_cs[q - 1:q, :] - a_cs)
    stack = jnp.concatenate([dt, exp_a, decay], axis=0)
    s_hi, s_lo = _split_bf16(stack)
    ex = (jnp.dot(s_hi, exp_ref[...], preferred_element_type=F32)
          + jnp.dot(s_lo, exp_ref[...], preferred_element_type=F32))
    dt_e = ex[0:q]
    exp_a_e = ex[q:2 * q]
    decay_e = ex[2 * q:3 * q]
    x_dt = xs * dt_e
    x_dec = x_dt * decay_e

    nt = (((1,), (1,)), ((), ()))
    cb = [lax.dot_general(cm[g], bm[g].astype(BF16), nt, preferred_element_type=F32)
          for g in range(SSD_GROUPS)]
    b_t = [bm[g].T.astype(BF16) for g in range(SSD_GROUPS)]
    causal = (lax.broadcasted_iota(jnp.int32, (q, q), 0) >= lax.broadcasted_iota(jnp.int32, (q, q), 1))
    lane = lax.broadcasted_iota(jnp.int32, (q, LANES), 1)
    first_head = lane < SSD_HEAD_DIM
    z = z_ref[0]
    heads_per_group = SSD_HEADS // SSD_GROUPS
    pairs = []
    for j in range(SSD_HEADS // 2):
        sl = slice(j * LANES, (j + 1) * LANES)
        g = (2 * j) // heads_per_group
        xp = x_dt[:, sl]
        y = jnp.zeros((q, LANES), F32)
        for hh in range(2):
            h = 2 * j + hh
            seg = a_cs[:, h:h + 1] - a_cs_t[h:h + 1, :]
            decay_l = jnp.exp(jnp.where(causal, seg, -jnp.inf))
            gm = (cb[g] * decay_l).astype(BF16)
            keep = first_head if hh == 0 else jnp.logical_not(first_head)
            xm = jnp.where(keep, xp, 0.0).astype(BF16)
            y = y + jnp.dot(gm, xm, preferred_element_type=F32)
        sp = st[j]
        y_off = jnp.dot(cm[g], sp.astype(BF16), preferred_element_type=F32) * exp_a_e[:, sl]
        st[j] = (sp * exp_a_e[q - 1:q, sl]
                 + jnp.dot(b_t[g], x_dec[:, sl].astype(BF16), preferred_element_type=F32))
        yp = y + y_off + dsk_ref[:, sl] * xs[:, sl]
        pairs.append(yp * _silu(z[:, sl]))
    y = jnp.concatenate(pairs, axis=1)
    gwid = SSD_INNER // SSD_GROUPS
    outs = []
    for g in range(SSD_GROUPS):
        yg = y[:, g * gwid:(g + 1) * gwid]
        ms = jnp.mean(yg * yg, axis=-1, keepdims=True)
        outs.append(yg * lax.rsqrt(ms + RMS_EPS) * ng_ref[:, g * gwid:(g + 1) * gwid])
    o_ref[0] = jnp.concatenate(outs, axis=1).astype(BF16)


def ssd_group(proj, conv_w, conv_b, dt_bias, a_log, d_skip, norm_g):
    b, s, _ = proj.shape
    q = SSD_CHUNK
    pad = lambda vec: jnp.pad(vec, (0, LANES - SSD_HEADS)).reshape(1, LANES)
    idx = jnp.arange(q)
    tri = (idx[:, None] >= idx[None, :]).astype(BF16)
    expand = (jnp.arange(LANES)[:, None] == (jnp.arange(SSD_INNER)[None, :] // SSD_HEAD_DIM)).astype(BF16)
    bcw = 2 * SSD_GROUPS * SSD_STATE
    const = lambda shape: pl.BlockSpec(shape, lambda bi, c: (0,) * len(shape))
    return pl.pallas_call(
        _ssd_kernel,
        grid=(b, s // q),
        in_specs=[
            pl.BlockSpec((1, q, SSD_INNER), lambda bi, c: (bi, c, _Z_BLK)),
            pl.BlockSpec((1, q, SSD_INNER), lambda bi, c: (bi, c, _XS_BLK)),
            pl.BlockSpec((1, q, bcw), lambda bi, c: (bi, c, _BC_BLK)),
            pl.BlockSpec((1, q, LANES), lambda bi, c: (bi, c, _DT_BLK)),
            const((CONV_WIDTH, SSD_INNER)), const((CONV_WIDTH, bcw)),
            const((1, SSD_INNER)), const((1, bcw)),
            const((1, LANES)), const((1, LANES)),
            const((1, SSD_INNER)), const((1, SSD_INNER)),
            const((q, q)), const((LANES, SSD_INNER)),
        ],
        out_specs=pl.BlockSpec((1, q, SSD_INNER), lambda bi, c: (bi, c, 0)),
        out_shape=jax.ShapeDtypeStruct((b, s, SSD_INNER), BF16),
        scratch_shapes=[pltpu.VMEM((SUBLANES, SSD_INNER), F32),
                        pltpu.VMEM((SUBLANES, bcw), F32),
                        pltpu.VMEM((SSD_HEADS // 2, SSD_STATE, LANES), F32)],
        compiler_params=_cparams(("parallel", "arbitrary")),
        name="ssd",
    )(proj, proj, proj, proj,
      conv_w[:, :SSD_INNER], conv_w[:, SSD_INNER:],
      conv_b[:SSD_INNER].reshape(1, -1), conv_b[SSD_INNER:].reshape(1, -1),
      pad(dt_bias), pad(a_log),
      jnp.repeat(d_skip, SSD_HEAD_DIM).reshape(1, -1), norm_g.reshape(1, -1),
      tri, expand)


def _out_proj_kernel(a_ref, s_ref, w_ref, x_ref, g_ref, o_ref):
    acc = jnp.dot(a_ref[0], w_ref[0:DA_WIDTH, :], preferred_element_type=F32)
    acc = acc + jnp.dot(s_ref[0], w_ref[DA_WIDTH:, :], preferred_element_type=F32)
    o_ref[0] = x_ref[0] + g_ref[0] * acc


def out_proj(attn, ssd, w, x, gate):
    b, s, d = x.shape
    tm = min(512, s)
    return pl.pallas_call(
        _out_proj_kernel,
        grid=(b, s // tm),
        in_specs=[
            pl.BlockSpec((1, tm, DA_WIDTH), lambda bi, i: (bi, i, 0)),
            pl.BlockSpec((1, tm, SSD_INNER), lambda bi, i: (bi, i, 0)),
            pl.BlockSpec((d, d), lambda bi, i: (0, 0)),
            pl.BlockSpec((1, tm, d), lambda bi, i: (bi, i, 0)),
            pl.BlockSpec((1, 1, d), lambda bi, i: (bi, 0, 0)),
        ],
        out_specs=pl.BlockSpec((1, tm, d), lambda bi, i: (bi, i, 0)),
        out_shape=jax.ShapeDtypeStruct((b, s, d), F32),
        compiler_params=_cparams(("parallel", "parallel")),
        name="out_proj",
    )(attn, ssd, w, x, gate.reshape(b, 1, d))


def _extract_topk(vals, pos, payload, k):
    n = vals.shape[1]
    rowk = lax.broadcasted_iota(jnp.int32, (k, n), 0)
    top_v = jnp.zeros((k, n), F32)
    top_p = jnp.zeros((k, n), F32)
    big = jnp.float32(1e9)
    for r in range(k):
        m = jnp.max(vals, axis=0, keepdims=True)
        p = jnp.min(jnp.where(vals == m, pos, big), axis=0, keepdims=True)
        sel = pos == p
        if payload is None:
            pay = p
        else:
            pay = jnp.max(jnp.where(sel, payload, -1.0), axis=0, keepdims=True)
        top_v = jnp.where(rowk == r, m, top_v)
        top_p = jnp.where(rowk == r, pay, top_p)
        vals = jnp.where(sel, -jnp.inf, vals)
    return top_v, top_p


def _route_kernel(x_ref, g_ref, sh_ref, sc_ref, wq_ref, k1_ref, k2_ref, h_ref, idx_ref, gate_ref):
    tm = x_ref.shape[1]
    h = _norm_modulate(x_ref[0], g_ref[...], sh_ref[0], sc_ref[0])
    for r in range(h.shape[1] // LANES):
        h_ref[:, r, :] = h[:, r * LANES:(r + 1) * LANES]
    qv = jnp.dot(h.astype(BF16), wq_ref[...], preferred_element_type=F32)
    nt = (((1,), (1,)), ((), ()))
    key_pos = lax.broadcasted_iota(jnp.int32, (PEER_N_KEYS, tm), 0).astype(F32)
    k = PEER_TOPK
    sub = lax.broadcasted_iota(jnp.int32, (SUBLANES, tm), 0).astype(F32)
    sub16 = lax.broadcasted_iota(jnp.int32, (k, tm), 0).astype(F32)
    idx_rows = []
    gate_rows = []
    for hd in range(PEER_HEADS):
        base = hd * 2 * PEER_HALF
        qa = qv[:, base:base + PEER_HALF].astype(BF16)
        qb = qv[:, base + PEER_HALF:base + 2 * PEER_HALF].astype(BF16)
        s1 = lax.dot_general(k1_ref[hd], qa, nt, preferred_element_type=F32)
        s2 = lax.dot_general(k2_ref[hd], qb, nt, preferred_element_type=F32)
        v1, i1 = _extract_topk(s1, key_pos, None, k)
        v2, i2 = _extract_topk(s2, key_pos, None, k)
        cv = [v1[0:1] + v2]
        ci = [i1[0:1] * PEER_N_KEYS + i2]
        cp = [sub16]
        for a in range(1, SUBLANES):
            cv.append(v1[a:a + 1] + v2[0:SUBLANES])
            ci.append(i1[a:a + 1] * PEER_N_KEYS + i2[0:SUBLANES])
            cp.append(sub + float(a * k))
        cv.append(v1[SUBLANES:k] + v2[0:1])
        ci.append(i1[SUBLANES:k] * PEER_N_KEYS + i2[0:1])
        cp.append((sub + float(SUBLANES)) * float(k))
        cand_v = jnp.concatenate(cv, axis=0)
        cand_i = jnp.concatenate(ci, axis=0)
        cand_p = jnp.concatenate(cp, axis=0)
        top, eidx = _extract_topk(cand_v, cand_p, cand_i, k)
        e = jnp.exp(top - jnp.max(top, axis=0, keepdims=True))
        gate_rows.append(e / jnp.sum(e, axis=0, keepdims=True))
        idx_rows.append(eidx)
    idx_ref[0] = jnp.concatenate(idx_rows, axis=0).T.astype(jnp.int32)
    gate_ref[0] = jnp.concatenate(gate_rows, axis=0).T


def peer_route(x, g, shift, scale, wq, k1, k2):
    b, s, d = x.shape
    tm = min(256, s)
    nsel = PEER_HEADS * PEER_TOPK
    const = lambda shape: pl.BlockSpec(shape, lambda bi, i: (0,) * len(shape))
    return pl.pallas_call(
        _route_kernel,
        grid=(b, s // tm),
        in_specs=[
            pl.BlockSpec((1, tm, d), lambda bi, i: (bi, i, 0)),
            const((1, d)),
            pl.BlockSpec((1, 1, d), lambda bi, i: (bi, 0, 0)),
            pl.BlockSpec((1, 1, d), lambda bi, i: (bi, 0, 0)),
            const((d, d)),
            const((PEER_HEADS, PEER_N_KEYS, PEER_HALF)),
            const((PEER_HEADS, PEER_N_KEYS, PEER_HALF)),
        ],
        out_specs=[pl.BlockSpec((tm, d // LANES, LANES), lambda bi, i: (bi * (s // tm) + i, 0, 0)),
                   pl.BlockSpec((1, tm, nsel), lambda bi, i: (bi, i, 0)),
                   pl.BlockSpec((1, tm, nsel), lambda bi, i: (bi, i, 0))],
        out_shape=[jax.ShapeDtypeStruct((b * s, d // LANES, LANES), F32),
                   jax.ShapeDtypeStruct((b, s, nsel), jnp.int32),
                   jax.ShapeDtypeStruct((b, s, nsel), F32)],
        compiler_params=_cparams(("parallel", "parallel")),
        name="peer_route",
    )(x, g.reshape(1, d), shift.reshape(b, 1, d), scale.reshape(b, 1, d), wq, k1, k2)


_ROW_TILES = D_MODEL // LANES
_NSEL = PEER_HEADS * PEER_TOPK
_GROUP = 8
_GROUPS_IN_FLIGHT = 4
_GROUPS_AHEAD = _GROUPS_IN_FLIGHT - 1
_PACK = 2 * SUBLANES
_BURST = 8
_VSUM = 4


def _expert_kernel(idx_ref, head_ref, gate_ref, h_ref, sel_ref, uv_hbm, o_ref, buf, sem, pscr, splat):
    te = h_ref.shape[0]
    n_groups = te // _GROUP
    ones = jnp.ones((_PACK, LANES), BF16)
    nt = (((1,), (1,)), ((), ()))

    def row_copies(index_of, g, slot):
        starts = []
        for i in range(_GROUP):
            t = g * _GROUP + i
            for k in range(_NSEL):
                starts.append(functools.partial(
                    lambda t, i, k: pltpu.make_async_copy(
                        uv_hbm.at[index_of(t, k)], buf.at[slot, i, k], sem.at[slot]).start(priority=k % 2),
                    t, i, k))
        return starts

    this_step = lambda t, k: idx_ref[t, k]
    next_step = lambda t, k: head_ref[0, t, k]

    def wait(slot):
        for i in range(_GROUP):
            pltpu.make_async_copy(uv_hbm.at[pl.ds(0, _NSEL)], buf.at[slot, i], sem.at[slot]).wait()

    def start_next(pending, n):
        for _ in range(n):
            start = next(pending, None)
            if start is not None:
                start()

    def u_products(g, slot, i, pending):
        t = g * _GROUP + i
        hb = h_ref[t].astype(BF16)
        for kk in range(_NSEL // 2):
            parts = []
            for k in (2 * kk, 2 * kk + 1):
                pr = (buf[slot, i, k, 0:_ROW_TILES, :] * hb).astype(F32)
                parts.append(pr[0:SUBLANES] + pr[SUBLANES:_ROW_TILES])
            pscr[i, kk * _PACK:(kk + 1) * _PACK, :] = jnp.concatenate(parts, axis=0).astype(BF16)
            if kk % _BURST == _BURST - 1:
                start_next(pending, _BURST)
        return jnp.dot(sel_ref[...], pscr[i], preferred_element_type=F32)

    def u_finish(g, i, m):
        t = g * _GROUP + i
        m_hi, m_lo = _split_bf16(m)
        a = (lax.dot_general(ones, m_hi, nt, preferred_element_type=F32)
             + lax.dot_general(ones, m_lo, nt, preferred_element_type=F32))[0:1]
        act = 0.5 * a * (1.0 + lax.erf(a * (2.0 ** -0.5))) * gate_ref[pl.ds(t, 1), :]
        hi = lax.shift_right_logical(pltpu.bitcast(act.astype(BF16).astype(F32), jnp.uint32), jnp.uint32(16))
        return pltpu.bitcast(lax.shift_left(hi, jnp.uint32(16)) | hi, F32)

    def v_token(g, slot, i, pending):
        t = g * _GROUP + i
        accs = [jnp.zeros((_ROW_TILES, LANES), F32) for _ in range(2)]
        for kq in range(_NSEL // _VSUM):
            part = None
            for k in range(kq * _VSUM, (kq + 1) * _VSUM):
                w = pltpu.bitcast(jnp.broadcast_to(splat[i, pl.ds(k, 1), :], (SUBLANES, LANES)), BF16)
                term = w * buf[slot, i, k, _ROW_TILES:2 * _ROW_TILES, :]
                part = term if part is None else part + term
            accs[kq % 2] = accs[kq % 2] + part.astype(F32)
            if kq % (2 * _BURST // _VSUM) == 2 * _BURST // _VSUM - 1:
                start_next(pending, _BURST)
        o_ref[t] = accs[0] + accs[1]

    def step(g_u, slot_u, g_v, slot_v, words, pending=iter(())):
        if words is not None:
            for i in range(_GROUP):
                splat[i] = jnp.broadcast_to(words[i], (LANES, LANES)).T
        nxt = []
        for i in range(_GROUP):
            m = u_products(g_u, slot_u, i, pending) if g_u is not None else None
            if words is not None:
                v_token(g_v, slot_v, i, pending)
            if m is not None:
                nxt.append(u_finish(g_u, i, m))
        return tuple(nxt)

    nif = _GROUPS_IN_FLIGHT
    ahead = _GROUPS_AHEAD
    grid_step = pl.program_id(0)
    slot_of = lambda g: lax.rem(grid_step * n_groups + g, nif)

    @pl.when(grid_step == 0)
    def _():
        for g0 in range(ahead):
            for start in row_copies(this_step, g0, g0):
                start()

    wait(slot_of(0))
    words0 = step(0, slot_of(0), None, None, None)

    def pipeline_step(index_of, g_issue, g, words):
        pending = iter(row_copies(index_of, g_issue, slot_of(g + ahead)))
        wait(slot_of(g + 1))
        nxt = step(g + 1, slot_of(g + 1), g, slot_of(g), words, pending)
        start_next(pending, _GROUP * _NSEL)
        return nxt

    steady = lambda g, words: pipeline_step(this_step, g + ahead, g, words)
    boundary = lambda g, words: pipeline_step(next_step, g + ahead - n_groups, g, words)

    zero = 0 * grid_step
    n_steady = n_groups - ahead
    words = lax.fori_loop(zero, zero + n_steady, steady, words0)
    words = lax.fori_loop(zero + n_steady, zero + (n_groups - 1), boundary, words)
    last = n_groups - 1
    pending = iter(row_copies(next_step, ahead - 1, slot_of(last + ahead)))
    step(None, None, last, slot_of(last), words, pending)
    start_next(pending, _GROUP * _NSEL)

    @pl.when(grid_step == pl.num_programs(0) - 1)
    def _():
        for j in range(ahead):
            wait(slot_of(n_groups + j))


def peer_experts(idx, gate, h3, uv):
    m = idx.shape[0]
    te = min(256, m)
    sel = (jnp.arange(_NSEL)[:, None] == jnp.arange(_NSEL * SUBLANES)[None, :] // SUBLANES).astype(BF16)
    n_steps = m // te
    n_head = _GROUPS_AHEAD * _GROUP
    head = idx.reshape(n_steps, te, _NSEL)[:, :n_head]
    return pl.pallas_call(
        _expert_kernel,
        grid=(n_steps,),
        in_specs=[
            pl.BlockSpec((te, _NSEL), lambda i: (i, 0), memory_space=pltpu.SMEM),
            pl.BlockSpec((1, n_head, _NSEL), lambda i: (jnp.minimum(i + 1, n_steps - 1), 0, 0),
                         memory_space=pltpu.SMEM),
            pl.BlockSpec((te, _NSEL), lambda i: (i, 0)),
            pl.BlockSpec((te, _ROW_TILES, LANES), lambda i: (i, 0, 0)),
            pl.BlockSpec((_NSEL, _NSEL * SUBLANES), lambda i: (0, 0)),
            pl.BlockSpec(memory_space=pl.ANY),
        ],
        out_specs=pl.BlockSpec((te, _ROW_TILES, LANES), lambda i: (i, 0, 0)),
        out_shape=jax.ShapeDtypeStruct((m, _ROW_TILES, LANES), F32),
        scratch_shapes=[
            pltpu.VMEM((_GROUPS_IN_FLIGHT, _GROUP, _NSEL, 2 * _ROW_TILES, LANES), BF16),
            pltpu.SemaphoreType.DMA((_GROUPS_IN_FLIGHT,)),
            pltpu.VMEM((_GROUP, _NSEL * SUBLANES, LANES), BF16),
            pltpu.VMEM((_GROUP, LANES, LANES), F32),
        ],
        compiler_params=_cparams(("arbitrary",)),
        name="peer_experts",
    )(idx, head, gate, h3, sel, uv)


def _residual_kernel(x_ref, y_ref, g_ref, o_ref):
    for r in range(_ROW_TILES):
        sl = slice(r * LANES, (r + 1) * LANES)
        o_ref[0, :, sl] = x_ref[0, :, sl] + g_ref[0, :, sl] * y_ref[:, r, :]


def gated_residual(x, y3, gate):
    b, s, d = x.shape
    tm = min(512, s)
    n_blk = s // tm
    blk = pl.BlockSpec((1, tm, d), lambda bi, i: (bi, i, 0))
    return pl.pallas_call(
        _residual_kernel,
        grid=(b, n_blk),
        in_specs=[blk, pl.BlockSpec((tm, _ROW_TILES, LANES), lambda bi, i: (bi * n_blk + i, 0, 0)),
                  pl.BlockSpec((1, 1, d), lambda bi, i: (bi, 0, 0))],
        out_specs=blk,
        out_shape=jax.ShapeDtypeStruct((b, s, d), F32),
        compiler_params=_cparams(("parallel", "parallel")),
        name="gated_residual",
    )(x, y3, gate.reshape(b, 1, d))


def _rope_tables(positions):
    inv_freq = ROPE_THETA ** (-jnp.arange(0, ROT_DIM, 2, dtype=F32) / ROT_DIM)
    ang = positions.astype(F32)[..., None] * inv_freq
    cos = jnp.cos(ang)
    sin = jnp.sin(ang)
    b, s, _ = ang.shape
    rest = DA_QK_DIM - ROT_DIM
    cos64 = jnp.concatenate([cos, cos, jnp.ones((b, s, rest), F32)], axis=-1)
    sin64 = jnp.concatenate([sin, sin, jnp.zeros((b, s, rest), F32)], axis=-1)
    return jnp.tile(cos64, (1, 1, LANES // DA_QK_DIM)), jnp.tile(sin64, (1, 1, LANES // DA_QK_DIM))


def kernel(x, c, positions, norm_mix_g, norm_ffn_g, w_ada, b_ada, w_in, q_norm_g, k_norm_g, lam_q1, lam_k1, lam_q2, lam_k2, subln_g, conv_w, conv_b, dt_bias, a_log, d_skip, ssd_norm_g, w_out, peer_wq, peer_k1, peer_k2, peer_u, peer_v):
    depth = w_ada.shape[0]
    b, s, d = x.shape
    cos_t, sin_t = _rope_tables(positions)
    mod = adaln(c, w_ada, b_ada)
    for i in range(depth):
        lam_init = 0.8 - 0.6 * math.exp(-0.3 * i)
        sh_m, sc_m, g_m, sh_f, sc_f, g_f = [mod[i, :, j * d:(j + 1) * d] for j in range(6)]
        w_in_p = jnp.pad(w_in[i], ((0, 0), (0, IN_DIM_PAD - IN_DIM))).astype(BF16)
        proj = in_proj(x, norm_mix_g[i], sh_m, sc_m, w_in_p)
        qn, kn, vn = qk_prep(proj, cos_t, sin_t, q_norm_g[i], k_norm_g[i])
        lam = (jnp.exp(jnp.sum(lam_q1[i] * lam_k1[i])) - jnp.exp(jnp.sum(lam_q2[i] * lam_k2[i])) + lam_init)
        attn = diff_attention(qn, kn, vn, lam, subln_g[i], lam_init)
        ssd = ssd_group(proj, conv_w[i], conv_b[i], dt_bias[i], a_log[i], d_skip[i], ssd_norm_g[i])
        x = out_proj(attn, ssd, w_out[i].astype(BF16), x, g_m)
        h2, idx, gate = peer_route(x, norm_ffn_g[i], sh_f, sc_f, peer_wq[i].astype(BF16),
                                   peer_k1[i].astype(BF16), peer_k2[i].astype(BF16))
        n_exp = peer_u.shape[1]
        uv = jnp.concatenate([peer_u[i].astype(BF16).reshape(n_exp, _ROW_TILES, LANES),
                              peer_v[i].astype(BF16).reshape(n_exp, _ROW_TILES, LANES)], axis=1)
        y3 = peer_experts(idx.reshape(b * s, _NSEL), gate.reshape(b * s, _NSEL),
                          h2, uv)
        x = gated_residual(x, y3, g_f)
    return x
```

```python
import functools
import math

import jax
import jax.numpy as jnp
from jax import lax
from jax.experimental import pallas as pl
from jax.experimental.pallas import tpu as pltpu

F32 = jnp.float32
BF16 = jnp.bfloat16

D_MODEL = 2048
DA_WIDTH = 1024
DA_HEADS = 8
DA_V_DIM = 128
DA_QK_DIM = 64
ROT_DIM = 16
ROPE_THETA = 500000.0
SSD_INNER = 1024
SSD_HEADS = 16
SSD_HEAD_DIM = 64
SSD_GROUPS = 2
SSD_STATE = 128
SSD_CHUNK = 128
CONV_WIDTH = 4
IN_DIM = 5648
PEER_HEADS = 8
PEER_N_KEYS = 128
PEER_HALF = 128
PEER_TOPK = 16
RMS_EPS = 1e-6

LANES = 128
SUBLANES = 8
IN_DIM_PAD = 5760
VMEM_LIMIT = 58 * 1024 * 1024

_Z_BLK = 3
_XS_BLK = 4
_BC_BLK = 10
_DT_BLK = 44

NEG_BIG = -1e30


def _cparams(sem):
    return pltpu.CompilerParams(dimension_semantics=sem, vmem_limit_bytes=VMEM_LIMIT)


def _adaln_kernel(c_ref, w_ref, b_ref, o_ref):
    c = c_ref[...]
    cond = (c * jax.nn.sigmoid(c)).astype(BF16)
    w = w_ref[0].astype(BF16)
    o_ref[0] = jnp.dot(cond, w, preferred_element_type=F32) + b_ref[0]


def adaln(c, w_ada, b_ada):
    depth, d, n = w_ada.shape
    b = c.shape[0]
    tn = 1024
    return pl.pallas_call(
        _adaln_kernel,
        grid=(depth, n // tn),
        in_specs=[
            pl.BlockSpec((b, d), lambda l, j: (0, 0)),
            pl.BlockSpec((1, d, tn), lambda l, j: (l, 0, j)),
            pl.BlockSpec((1, 1, tn), lambda l, j: (l, 0, j)),
        ],
        out_specs=pl.BlockSpec((1, b, tn), lambda l, j: (l, 0, j)),
        out_shape=jax.ShapeDtypeStruct((depth, b, n), F32),
        compiler_params=_cparams(("parallel", "parallel")),
        name="adaln",
    )(c, w_ada, b_ada.reshape(depth, 1, n))


def _norm_modulate(x, g, shift, scale):
    ms = jnp.mean(x * x, axis=-1, keepdims=True)
    y = x * lax.rsqrt(ms + RMS_EPS) * g
    return y * (1.0 + scale) + shift


def _in_proj_kernel(x_ref, g_ref, sh_ref, sc_ref, w_ref, o_ref, h_scr):
    @pl.when(pl.program_id(2) == 0)
    def _():
        h = _norm_modulate(x_ref[0], g_ref[...], sh_ref[0], sc_ref[0])
        h_scr[...] = h.astype(BF16)

    o_ref[0] = jnp.dot(h_scr[...], w_ref[...], preferred_element_type=F32)


def in_proj(x, g, shift, scale, w):
    b, s, d = x.shape
    n = w.shape[1]
    tm = min(1024, s)
    tn = 1152
    return pl.pallas_call(
        _in_proj_kernel,
        grid=(b, s // tm, n // tn),
        in_specs=[
            pl.BlockSpec((1, tm, d), lambda bi, i, j: (bi, i, 0)),
            pl.BlockSpec((1, d), lambda bi, i, j: (0, 0)),
            pl.BlockSpec((1, 1, d), lambda bi, i, j: (bi, 0, 0)),
            pl.BlockSpec((1, 1, d), lambda bi, i, j: (bi, 0, 0)),
            pl.BlockSpec((d, tn), lambda bi, i, j: (0, j)),
        ],
        out_specs=pl.BlockSpec((1, tm, tn), lambda bi, i, j: (bi, i, j)),
        out_shape=jax.ShapeDtypeStruct((b, s, n), F32),
        scratch_shapes=[pltpu.VMEM((tm, d), BF16)],
        compiler_params=_cparams(("parallel", "parallel", "arbitrary")),
        name="in_proj",
    )(x, g.reshape(1, d), shift.reshape(b, 1, d), scale.reshape(b, 1, d), w)


def _qk_prep_kernel(q_ref, k_ref, v_ref, cos_ref, sin_ref, gq_ref, gk_ref, seg_ref, rot_ref,
                    qo_ref, ko_ref, vo_ref):
    cos = cos_ref[0]
    sin = sin_ref[0]
    seg = seg_ref[...]
    rot_m = rot_ref[...]

    def prep(x_ref, g, o_ref, out_scale):
        for c in range(DA_WIDTH // LANES):
            sl = slice(c * LANES, (c + 1) * LANES)
            x = x_ref[0, :, sl]
            ss = jnp.dot((x * x).astype(BF16), seg, preferred_element_type=F32)
            y = x * lax.rsqrt(ss * (1.0 / DA_QK_DIM) + RMS_EPS) * g
            r = jnp.dot(y.astype(BF16), rot_m, preferred_element_type=F32)
            o_ref[0, :, sl] = ((y * cos + r * sin) * out_scale).astype(BF16)

    prep(q_ref, gq_ref[...], qo_ref, DA_QK_DIM ** -0.5)
    prep(k_ref, gk_ref[...], ko_ref, 1.0)
    vo_ref[0] = v_ref[0].astype(BF16)


def qk_prep(proj, cos_t, sin_t, gq, gk):
    b, s, _ = proj.shape
    ts = min(512, s)
    lane = jnp.arange(LANES)
    seg_m = (lane[:, None] // DA_QK_DIM == lane[None, :] // DA_QK_DIM).astype(BF16)
    off = lane % DA_QK_DIM
    half = ROT_DIM // 2
    src = lane[:, None]
    dst = lane[None, :]
    rot_m = (jnp.where((off[None, :] < half) & (src == dst + half), -1.0, 0.0)
             + jnp.where((off[None, :] >= half) & (off[None, :] < ROT_DIM) & (src == dst - half), 1.0, 0.0)
             ).astype(BF16)
    gq_t = jnp.tile(gq, LANES // DA_QK_DIM).reshape(1, LANES)
    gk_t = jnp.tile(gk, LANES // DA_QK_DIM).reshape(1, LANES)
    blk = lambda c: pl.BlockSpec((1, ts, DA_WIDTH), lambda bi, i, c=c: (bi, i, c))
    const = lambda shape: pl.BlockSpec(shape, lambda bi, i: (0,) * len(shape))
    out_sds = jax.ShapeDtypeStruct((b, s, DA_WIDTH), BF16)
    return pl.pallas_call(
        _qk_prep_kernel,
        grid=(b, s // ts),
        in_specs=[blk(0), blk(1), blk(2),
                  pl.BlockSpec((1, ts, LANES), lambda bi, i: (bi, i, 0)),
                  pl.BlockSpec((1, ts, LANES), lambda bi, i: (bi, i, 0)),
                  const((1, LANES)), const((1, LANES)),
                  const((LANES, LANES)), const((LANES, LANES))],
        out_specs=[pl.BlockSpec((1, ts, DA_WIDTH), lambda bi, i: (bi, i, 0))] * 3,
        out_shape=[out_sds] * 3,
        compiler_params=_cparams(("parallel", "parallel")),
        name="qk_prep",
    )(proj, proj, proj, cos_t, sin_t, gq_t, gk_t, seg_m, rot_m)


def _attn_kernel(lam_ref, q_ref, k_ref, v_ref, g_ref, o_ref, *, tq, tk, out_scale):
    qi = pl.program_id(2)
    q = q_ref[0]
    lane = lax.broadcasted_iota(jnp.int32, (tq, LANES), 1)
    zero = jnp.zeros_like(q)
    q0 = jnp.where(lane < DA_QK_DIM, q, zero)
    q1 = jnp.where(lane >= DA_QK_DIM, q, zero)
    row = qi * tq + lax.broadcasted_iota(jnp.int32, (tq, tk), 0)
    col0 = lax.broadcasted_iota(jnp.int32, (tq, tk), 1)
    nt = (((1,), (1,)), ((), ()))

    def update(s, m, l, acc, v):
        m_new = jnp.maximum(m, jnp.max(s, axis=-1, keepdims=True))
        p = jnp.exp(s - m_new)
        alpha = jnp.exp(m - m_new)
        l = alpha * l + jnp.sum(p, axis=-1, keepdims=True)
        acc = alpha * acc + jnp.dot(p.astype(BF16), v, preferred_element_type=F32)
        return m_new, l, acc

    def body(j, carry):
        m0, l0, a0, m1, l1, a1 = carry
        start = pl.multiple_of(j * tk, tk)
        k = k_ref[0, pl.ds(start, tk), :]
        v = v_ref[0, pl.ds(start, tk), :]
        ok = (col0 + j * tk) <= row
        s0 = jnp.where(ok, lax.dot_general(q0, k, nt, preferred_element_type=F32), NEG_BIG)
        s1 = jnp.where(ok, lax.dot_general(q1, k, nt, preferred_element_type=F32), NEG_BIG)
        m0, l0, a0 = update(s0, m0, l0, a0, v)
        m1, l1, a1 = update(s1, m1, l1, a1, v)
        return m0, l0, a0, m1, l1, a1

    mi = jnp.full((tq, 1), NEG_BIG, F32)
    li = jnp.zeros((tq, 1), F32)
    ai = jnp.zeros((tq, DA_V_DIM), F32)
    n_kv = ((qi + 1) * tq + tk - 1) // tk
    m0, l0, a0, m1, l1, a1 = lax.fori_loop(0, n_kv, body, (mi, li, ai, mi, li, ai))
    out = a0 / l0 - lam_ref[0] * (a1 / l1)
    ms = jnp.mean(out * out, axis=-1, keepdims=True)
    o_ref[0] = (out * lax.rsqrt(ms + RMS_EPS) * g_ref[...] * out_scale).astype(BF16)


def diff_attention(qn, kn, vn, lam, subln_g, lam_init):
    b, s, _ = qn.shape
    tq = min(512, s)
    tk = min(512, s)
    kern = functools.partial(_attn_kernel, tq=tq, tk=tk, out_scale=1.0 - lam_init)
    return pl.pallas_call(
        kern,
        grid=(b, DA_HEADS, s // tq),
        in_specs=[
            pl.BlockSpec(memory_space=pltpu.SMEM),
            pl.BlockSpec((1, tq, LANES), lambda bi, h, i: (bi, i, h)),
            pl.BlockSpec((1, s, LANES), lambda bi, h, i: (bi, 0, h)),
            pl.BlockSpec((1, s, LANES), lambda bi, h, i: (bi, 0, h)),
            pl.BlockSpec((1, LANES), lambda bi, h, i: (0, 0)),
        ],
        out_specs=pl.BlockSpec((1, tq, LANES), lambda bi, h, i: (bi, i, h)),
        out_shape=jax.ShapeDtypeStruct((b, s, DA_WIDTH), BF16),
        compiler_params=_cparams(("parallel", "parallel", "parallel")),
        name="diff_attn",
    )(lam.reshape(1).astype(F32), qn, kn, vn, subln_g.reshape(1, LANES))


def _split_bf16(x):
    hi = x.astype(BF16)
    lo = (x - hi.astype(F32)).astype(BF16)
    return hi, lo


def _silu(x):
    return x * jax.nn.sigmoid(x)


def _ssd_kernel(z_ref, xs_ref, bc_ref, dt_ref, cwx_ref, cwb_ref, cbx_ref, cbb_ref, dtb_ref,
                alog_ref, dsk_ref, ng_ref, tri_ref, exp_ref, o_ref, pxs, pbc, st):
    q = SSD_CHUNK

    @pl.when(pl.program_id(1) == 0)
    def _():
        pxs[...] = jnp.zeros_like(pxs)
        pbc[...] = jnp.zeros_like(pbc)
        st[...] = jnp.zeros_like(st)

    def conv(x, prev, w_ref, b_ref):
        acc = x * w_ref[CONV_WIDTH - 1:CONV_WIDTH, :] + b_ref[...]
        row8 = lax.broadcasted_iota(jnp.int32, (SUBLANES, x.shape[1]), 0)
        for k in range(1, CONV_WIDTH):
            xr = pltpu.roll(x, k, 0)
            pr = pltpu.roll(prev, k, 0)
            first = jnp.where(row8 < k, pr, xr[0:SUBLANES])
            xk = jnp.concatenate([first, xr[SUBLANES:]], axis=0)
            acc = acc + xk * w_ref[CONV_WIDTH - 1 - k:CONV_WIDTH - k, :]
        return acc

    xs_raw = xs_ref[0]
    bc_raw = bc_ref[0]
    xs = _silu(conv(xs_raw, pxs[...], cwx_ref, cbx_ref))
    bcs = _silu(conv(bc_raw, pbc[...], cwb_ref, cbb_ref))
    pxs[...] = xs_raw[q - SUBLANES:q]
    pbc[...] = bc_raw[q - SUBLANES:q]

    gw = SSD_GROUPS * SSD_STATE
    bm = [bcs[:, g * SSD_STATE:(g + 1) * SSD_STATE] for g in range(SSD_GROUPS)]
    cm = [bcs[:, gw + g * SSD_STATE:gw + (g + 1) * SSD_STATE].astype(BF16) for g in range(SSD_GROUPS)]

    v = dt_ref[0] + dtb_ref[...]
    dt = jnp.maximum(v, 0.0) + jnp.log1p(jnp.exp(-jnp.abs(v)))
    a_dt = dt * (-jnp.exp(alog_ref[...]))
    tri = tri_ref[...]
    a_hi, a_lo = _split_bf16(a_dt)
    a_cs = (jnp.dot(tri, a_hi, preferred_element_type=F32)
            + jnp.dot(tri, a_lo, preferred_element_type=F32))
    a_cs_t = a_cs.T
    exp_a = jnp.exp(a_cs)
    decay = jnp.exp(a_cs[q - 1:q, :] - a_cs)
    stack = jnp.concatenate([dt, exp_a, decay], axis=0)
    s_hi, s_lo = _split_bf16(stack)
    ex = (jnp.dot(s_hi, exp_ref[...], preferred_element_type=F32)
          + jnp.dot(s_lo, exp_ref[...], preferred_element_type=F32))
    dt_e = ex[0:q]
    exp_a_e = ex[q:2 * q]
    decay_e = ex[2 * q:3 * q]
    x_dt = xs * dt_e
    x_dec = x_dt * decay_e

    nt = (((1,), (1,)), ((), ()))
    cb = [lax.dot_general(cm[g], bm[g].astype(BF16), nt, preferred_element_type=F32)
          for g in range(SSD_GROUPS)]
    b_t = [bm[g].T.astype(BF16) for g in range(SSD_GROUPS)]
    causal = (lax.broadcasted_iota(jnp.int32, (q, q), 0) >= lax.broadcasted_iota(jnp.int32, (q, q), 1))
    lane = lax.broadcasted_iota(jnp.int32, (q, LANES), 1)
    first_head = lane < SSD_HEAD_DIM
    z = z_ref[0]
    heads_per_group = SSD_HEADS // SSD_GROUPS
    pairs = []
    for j in range(SSD_HEADS // 2):
        sl = slice(j * LANES, (j + 1) * LANES)
        g = (2 * j) // heads_per_group
        xp = x_dt[:, sl]
        y = jnp.zeros((q, LANES), F32)
        for hh in range(2):
            h = 2 * j + hh
            seg = a_cs[:, h:h + 1] - a_cs_t[h:h + 1, :]
            decay_l = jnp.exp(jnp.where(causal, seg, -jnp.inf))
            gm = (cb[g] * decay_l).astype(BF16)
            keep = first_head if hh == 0 else jnp.logical_not(first_head)
            xm = jnp.where(keep, xp, 0.0).astype(BF16)
            y = y + jnp.dot(gm, xm, preferred_element_type=F32)
        sp = st[j]
        y_off = jnp.dot(cm[g], sp.astype(BF16), preferred_element_type=F32) * exp_a_e[:, sl]
        st[j] = (sp * exp_a_e[q - 1:q, sl]
                 + jnp.dot(b_t[g], x_dec[:, sl].astype(BF16), preferred_element_type=F32))
        yp = y + y_off + dsk_ref[:, sl] * xs[:, sl]
        pairs.append(yp * _silu(z[:, sl]))
    y = jnp.concatenate(pairs, axis=1)
    gwid = SSD_INNER // SSD_GROUPS
    outs = []
    for g in range(SSD_GROUPS):
        yg = y[:, g * gwid:(g + 1) * gwid]
        ms = jnp.mean(yg * yg, axis=-1, keepdims=True)
        outs.append(yg * lax.rsqrt(ms + RMS_EPS) * ng_ref[:, g * gwid:(g + 1) * gwid])
    o_ref[0] = jnp.concatenate(outs, axis=1).astype(BF16)


def ssd_group(proj, conv_w, conv_b, dt_bias, a_log, d_skip, norm_g):
    b, s, _ = proj.shape
    q = SSD_CHUNK
    pad = lambda vec: jnp.pad(vec, (0, LANES - SSD_HEADS)).reshape(1, LANES)
    idx = jnp.arange(q)
    tri = (idx[:, None] >= idx[None, :]).astype(BF16)
    expand = (jnp.arange(LANES)[:, None] == (jnp.arange(SSD_INNER)[None, :] // SSD_HEAD_DIM)).astype(BF16)
    bcw = 2 * SSD_GROUPS * SSD_STATE
    const = lambda shape: pl.BlockSpec(shape, lambda bi, c: (0,) * len(shape))
    return pl.pallas_call(
        _ssd_kernel,
        grid=(b, s // q),
        in_specs=[
            pl.BlockSpec((1, q, SSD_INNER), lambda bi, c: (bi, c, _Z_BLK)),
            pl.BlockSpec((1, q, SSD_INNER), lambda bi, c: (bi, c, _XS_BLK)),
            pl.BlockSpec((1, q, bcw), lambda bi, c: (bi, c, _BC_BLK)),
            pl.BlockSpec((1, q, LANES), lambda bi, c: (bi, c, _DT_BLK)),
            const((CONV_WIDTH, SSD_INNER)), const((CONV_WIDTH, bcw)),
            const((1, SSD_INNER)), const((1, bcw)),
            const((1, LANES)), const((1, LANES)),
            const((1, SSD_INNER)), const((1, SSD_INNER)),
            const((q, q)), const((LANES, SSD_INNER)),
        ],
        out_specs=pl.BlockSpec((1, q, SSD_INNER), lambda bi, c: (bi, c, 0)),
        out_shape=jax.ShapeDtypeStruct((b, s, SSD_INNER), BF16),
        scratch_shapes=[pltpu.VMEM((SUBLANES, SSD_INNER), F32),
                        pltpu.VMEM((SUBLANES, bcw), F32),
                        pltpu.VMEM((SSD_HEADS // 2, SSD_STATE, LANES), F32)],
        compiler_params=_cparams(("parallel", "arbitrary")),
        name="ssd",
    )(proj, proj, proj, proj,
      conv_w[:, :SSD_INNER], conv_w[:, SSD_INNER:],
      conv_b[:SSD_INNER].reshape(1, -1), conv_b[SSD_INNER:].reshape(1, -1),
      pad(dt_bias), pad(a_log),
      jnp.repeat(d_skip, SSD_HEAD_DIM).reshape(1, -1), norm_g.reshape(1, -1),
      tri, expand)


def _out_proj_kernel(a_ref, s_ref, w_ref, x_ref, g_ref, o_ref):
    acc = jnp.dot(a_ref[0], w_ref[0:DA_WIDTH, :], preferred_element_type=F32)
    acc = acc + jnp.dot(s_ref[0], w_ref[DA_WIDTH:, :], preferred_element_type=F32)
    o_ref[0] = x_ref[0] + g_ref[0] * acc


def out_proj(attn, ssd, w, x, gate):
    b, s, d = x.shape
    tm = min(512, s)
    return pl.pallas_call(
        _out_proj_kernel,
        grid=(b, s // tm),
        in_specs=[
            pl.BlockSpec((1, tm, DA_WIDTH), lambda bi, i: (bi, i, 0)),
            pl.BlockSpec((1, tm, SSD_INNER), lambda bi, i: (bi, i, 0)),
            pl.BlockSpec((d, d), lambda bi, i: (0, 0)),
            pl.BlockSpec((1, tm, d), lambda bi, i: (bi, i, 0)),
            pl.BlockSpec((1, 1, d), lambda bi, i: (bi, 0, 0)),
        ],
        out_specs=pl.BlockSpec((1, tm, d), lambda bi, i: (bi, i, 0)),
        out_shape=jax.ShapeDtypeStruct((b, s, d), F32),
        compiler_params=_cparams(("parallel", "parallel")),
        name="out_proj",
    )(attn, ssd, w, x, gate.reshape(b, 1, d))


def _extract_topk(vals, pos, payload, k):
    n = vals.shape[1]
    rowk = lax.broadcasted_iota(jnp.int32, (k, n), 0)
    top_v = jnp.zeros((k, n), F32)
    top_p = jnp.zeros((k, n), F32)
    big = jnp.float32(1e9)
    for r in range(k):
        m = jnp.max(vals, axis=0, keepdims=True)
        p = jnp.min(jnp.where(vals == m, pos, big), axis=0, keepdims=True)
        sel = pos == p
        if payload is None:
            pay = p
        else:
            pay = jnp.max(jnp.where(sel, payload, -1.0), axis=0, keepdims=True)
        top_v = jnp.where(rowk == r, m, top_v)
        top_p = jnp.where(rowk == r, pay, top_p)
        vals = jnp.where(sel, -jnp.inf, vals)
    return top_v, top_p


def _route_kernel(x_ref, g_ref, sh_ref, sc_ref, wq_ref, k1_ref, k2_ref, h_ref, idx_ref, gate_ref):
    tm = x_ref.shape[1]
    h = _norm_modulate(x_ref[0], g_ref[...], sh_ref[0], sc_ref[0])
    for r in range(h.shape[1] // LANES):
        h_ref[:, r, :] = h[:, r * LANES:(r + 1) * LANES]
    qv = jnp.dot(h.astype(BF16), wq_ref[...], preferred_element_type=F32)
    nt = (((1,), (1,)), ((), ()))
    key_pos = lax.broadcasted_iota(jnp.int32, (PEER_N_KEYS, tm), 0).astype(F32)
    k = PEER_TOPK
    sub = lax.broadcasted_iota(jnp.int32, (SUBLANES, tm), 0).astype(F32)
    sub16 = lax.broadcasted_iota(jnp.int32, (k, tm), 0).astype(F32)
    idx_rows = []
    gate_rows = []
    for hd in range(PEER_HEADS):
        base = hd * 2 * PEER_HALF
        qa = qv[:, base:base + PEER_HALF].astype(BF16)
        qb = qv[:, base + PEER_HALF:base + 2 * PEER_HALF].astype(BF16)
        s1 = lax.dot_general(k1_ref[hd], qa, nt, preferred_element_type=F32)
        s2 = lax.dot_general(k2_ref[hd], qb, nt, preferred_element_type=F32)
        v1, i1 = _extract_topk(s1, key_pos, None, k)
        v2, i2 = _extract_topk(s2, key_pos, None, k)
        cv = [v1[0:1] + v2]
        ci = [i1[0:1] * PEER_N_KEYS + i2]
        cp = [sub16]
        for a in range(1, SUBLANES):
            cv.append(v1[a:a + 1] + v2[0:SUBLANES])
            ci.append(i1[a:a + 1] * PEER_N_KEYS + i2[0:SUBLANES])
            cp.append(sub + float(a * k))
        cv.append(v1[SUBLANES:k] + v2[0:1])
        ci.append(i1[SUBLANES:k] * PEER_N_KEYS + i2[0:1])
        cp.append((sub + float(SUBLANES)) * float(k))
        cand_v = jnp.concatenate(cv, axis=0)
        cand_i = jnp.concatenate(ci, axis=0)
        cand_p = jnp.concatenate(cp, axis=0)
        top, eidx = _extract_topk(cand_v, cand_p, cand_i, k)
        e = jnp.exp(top - jnp.max(top, axis=0, keepdims=True))
        gate_rows.append(e / jnp.sum(e, axis=0, keepdims=True))
        idx_rows.append(eidx)
    idx_ref[0] = jnp.concatenate(idx_rows, axis=0).T.astype(jnp.int32)
    gate_ref[0] = jnp.concatenate(gate_rows, axis=0).T


def peer_route(x, g, shift, scale, wq, k1, k2):
    b, s, d = x.shape
    tm = min(256, s)
    nsel = PEER_HEADS * PEER_TOPK
    const = lambda shape: pl.BlockSpec(shape, lambda bi, i: (0,) * len(shape))
    return pl.pallas_call(
        _route_kernel,
        grid=(b, s // tm),
        in_specs=[
            pl.BlockSpec((1, tm, d), lambda bi, i: (bi, i, 0)),
            const((1, d)),
            pl.BlockSpec((1, 1, d), lambda bi, i: (bi, 0, 0)),
            pl.BlockSpec((1, 1, d), lambda bi, i: (bi, 0, 0)),
            const((d, d)),
            const((PEER_HEADS, PEER_N_KEYS, PEER_HALF)),
            const((PEER_HEADS, PEER_N_KEYS, PEER_HALF)),
        ],
        out_specs=[pl.BlockSpec((tm, d // LANES, LANES), lambda bi, i: (bi * (s // tm) + i, 0, 0)),
                   pl.BlockSpec((1, tm, nsel), lambda bi, i: (bi, i, 0)),
                   pl.BlockSpec((1, tm, nsel), lambda bi, i: (bi, i, 0))],
        out_shape=[jax.ShapeDtypeStruct((b * s, d // LANES, LANES), F32),
                   jax.ShapeDtypeStruct((b, s, nsel), jnp.int32),
                   jax.ShapeDtypeStruct((b, s, nsel), F32)],
        compiler_params=_cparams(("parallel", "parallel")),
        name="peer_route",
    )(x, g.reshape(1, d), shift.reshape(b, 1, d), scale.reshape(b, 1, d), wq, k1, k2)


_ROW_TILES = D_MODEL // LANES
_NSEL = PEER_HEADS * PEER_TOPK
_GROUP = 8
_GROUPS_IN_FLIGHT = 4
_GROUPS_AHEAD = _GROUPS_IN_FLIGHT - 1
_PACK = 2 * SUBLANES
_BURST = 8
_VSUM = 4


def _expert_kernel(idx_ref, head_ref, gate_ref, h_ref, sel_ref, uv_hbm, o_ref, buf, sem, pscr, splat):
    te = h_ref.shape[0]
    n_groups = te // _GROUP
    ones = jnp.ones((_PACK, LANES), BF16)
    nt = (((1,), (1,)), ((), ()))

    def row_copies(index_of, g, slot):
        starts = []
        for i in range(_GROUP):
            t = g * _GROUP + i
            for k in range(_NSEL):
                starts.append(functools.partial(
                    lambda t, i, k: pltpu.make_async_copy(
                        uv_hbm.at[index_of(t, k)], buf.at[slot, i, k], sem.at[slot]).start(priority=k % 2),
                    t, i, k))
        return starts

    this_step = lambda t, k: idx_ref[t, k]
    next_step = lambda t, k: head_ref[0, t, k]

    def wait(slot):
        for i in range(_GROUP):
            pltpu.make_async_copy(uv_hbm.at[pl.ds(0, _NSEL)], buf.at[slot, i], sem.at[slot]).wait()

    def start_next(pending, n):
        for _ in range(n):
            start = next(pending, None)
            if start is not None:
                start()

    def u_products(g, slot, i, pending):
        t = g * _GROUP + i
        hb = h_ref[t].astype(BF16)
        for kk in range(_NSEL // 2):
            parts = []
            for k in (2 * kk, 2 * kk + 1):
                pr = (buf[slot, i, k, 0:_ROW_TILES, :] * hb).astype(F32)
                parts.append(pr[0:SUBLANES] + pr[SUBLANES:_ROW_TILES])
            pscr[i, kk * _PACK:(kk + 1) * _PACK, :] = jnp.concatenate(parts, axis=0).astype(BF16)
            if kk % _BURST == _BURST - 1:
                start_next(pending, _BURST)
        return jnp.dot(sel_ref[...], pscr[i], preferred_element_type=F32)

    def u_finish(g, i, m):
        t = g * _GROUP + i
        m_hi, m_lo = _split_bf16(m)
        a = (lax.dot_general(ones, m_hi, nt, preferred_element_type=F32)
             + lax.dot_general(ones, m_lo, nt, preferred_element_type=F32))[0:1]
        act = 0.5 * a * (1.0 + lax.erf(a * (2.0 ** -0.5))) * gate_ref[pl.ds(t, 1), :]
        hi = lax.shift_right_logical(pltpu.bitcast(act.astype(BF16).astype(F32), jnp.uint32), jnp.uint32(16))
        return pltpu.bitcast(lax.shift_left(hi, jnp.uint32(16)) | hi, F32)

    def v_token(g, slot, i, pending):
        t = g * _GROUP + i
        accs = [jnp.zeros((_ROW_TILES, LANES), F32) for _ in range(2)]
        for kq in range(_NSEL // _VSUM):
            part = None
            for k in range(kq * _VSUM, (kq + 1) * _VSUM):
                w = pltpu.bitcast(jnp.broadcast_to(splat[i, pl.ds(k, 1), :], (SUBLANES, LANES)), BF16)
                term = w * buf[slot, i, k, _ROW_TILES:2 * _ROW_TILES, :]
                part = term if part is None else part + term
            accs[kq % 2] = accs[kq % 2] + part.astype(F32)
            if kq % (2 * _BURST // _VSUM) == 2 * _BURST // _VSUM - 1:
                start_next(pending, _BURST)
        o_ref[t] = accs[0] + accs[1]

    def step(g_u, slot_u, g_v, slot_v, words, pending=iter(())):
        if words is not None:
            for i in range(_GROUP):
                splat[i] = jnp.broadcast_to(words[i], (LANES, LANES)).T
        nxt = []
        for i in range(_GROUP):
            m = u_products(g_u, slot_u, i, pending) if g_u is not None else None
            if words is not None:
                v_token(g_v, slot_v, i, pending)
            if m is not None:
                nxt.append(u_finish(g_u, i, m))
        return tuple(nxt)

    nif = _GROUPS_IN_FLIGHT
    ahead = _GROUPS_AHEAD
    grid_step = pl.program_id(0)
    slot_of = lambda g: lax.rem(grid_step * n_groups + g, nif)

    @pl.when(grid_step == 0)
    def _():
        for g0 in range(ahead):
            for start in row_copies(this_step, g0, g0):
                start()

    wait(slot_of(0))
    words0 = step(0, slot_of(0), None, None, None)

    def pipeline_step(index_of, g_issue, g, words):
        pending = iter(row_copies(index_of, g_issue, slot_of(g + ahead)))
        wait(slot_of(g + 1))
        nxt = step(g + 1, slot_of(g + 1), g, slot_of(g), words, pending)
        start_next(pending, _GROUP * _NSEL)
        return nxt

    steady = lambda g, words: pipeline_step(this_step, g + ahead, g, words)
    boundary = lambda g, words: pipeline_step(next_step, g + ahead - n_groups, g, words)

    zero = 0 * grid_step
    n_steady = n_groups - ahead
    words = lax.fori_loop(zero, zero + n_steady, steady, words0)
    words = lax.fori_loop(zero + n_steady, zero + (n_groups - 1), boundary, words)
    last = n_groups - 1
    pending = iter(row_copies(next_step, ahead - 1, slot_of(last + ahead)))
    step(None, None, last, slot_of(last), words, pending)
    start_next(pending, _GROUP * _NSEL)

    @pl.when(grid_step == pl.num_programs(0) - 1)
    def _():
        for j in range(ahead):
            wait(slot_of(n_groups + j))


def peer_experts(idx, gate, h3, uv):
    m = idx.shape[0]
    te = min(256, m)
    sel = (jnp.arange(_NSEL)[:, None] == jnp.arange(_NSEL * SUBLANES)[None, :] // SUBLANES).astype(BF16)
    n_steps = m // te
    n_head = _GROUPS_AHEAD * _GROUP
    head = idx.reshape(n_steps, te, _NSEL)[:, :n_head]
    return pl.pallas_call(
        _expert_kernel,
        grid=(n_steps,),
        in_specs=[
            pl.BlockSpec((te, _NSEL), lambda i: (i, 0), memory_space=pltpu.SMEM),
            pl.BlockSpec((1, n_head, _NSEL), lambda i: (jnp.minimum(i + 1, n_steps - 1), 0, 0),
                         memory_space=pltpu.SMEM),
            pl.BlockSpec((te, _NSEL), lambda i: (i, 0)),
            pl.BlockSpec((te, _ROW_TILES, LANES), lambda i: (i, 0, 0)),
            pl.BlockSpec((_NSEL, _NSEL * SUBLANES), lambda i: (0, 0)),
            pl.BlockSpec(memory_space=pl.ANY),
        ],
        out_specs=pl.BlockSpec((te, _ROW_TILES, LANES), lambda i: (i, 0, 0)),
        out_shape=jax.ShapeDtypeStruct((m, _ROW_TILES, LANES), F32),
        scratch_shapes=[
            pltpu.VMEM((_GROUPS_IN_FLIGHT, _GROUP, _NSEL, 2 * _ROW_TILES, LANES), BF16),
            pltpu.SemaphoreType.DMA((_GROUPS_IN_FLIGHT,)),
            pltpu.VMEM((_GROUP, _NSEL * SUBLANES, LANES), BF16),
            pltpu.VMEM((_GROUP, LANES, LANES), F32),
        ],
        compiler_params=_cparams(("arbitrary",)),
        name="peer_experts",
    )(idx, head, gate, h3, sel, uv)


def _residual_kernel(x_ref, y_ref, g_ref, o_ref):
    for r in range(_ROW_TILES):
        sl = slice(r * LANES, (r + 1) * LANES)
        o_ref[0, :, sl] = x_ref[0, :, sl] + g_ref[0, :, sl] * y_ref[:, r, :]


def gated_residual(x, y3, gate):
    b, s, d = x.shape
    tm = min(512, s)
    n_blk = s // tm
    blk = pl.BlockSpec((1, tm, d), lambda bi, i: (bi, i, 0))
    return pl.pallas_call(
        _residual_kernel,
        grid=(b, n_blk),
        in_specs=[blk, pl.BlockSpec((tm, _ROW_TILES, LANES), lambda bi, i: (bi * n_blk + i, 0, 0)),
                  pl.BlockSpec((1, 1, d), lambda bi, i: (bi, 0, 0))],
        out_specs=blk,
        out_shape=jax.ShapeDtypeStruct((b, s, d), F32),
        compiler_params=_cparams(("parallel", "parallel")),
        name="gated_residual",
    )(x, y3, gate.reshape(b, 1, d))


def _rope_tables(positions):
    inv_freq = ROPE_THETA ** (-jnp.arange(0, ROT_DIM, 2, dtype=F32) / ROT_DIM)
    ang = positions.astype(F32)[..., None] * inv_freq
    cos = jnp.cos(ang)
    sin = jnp.sin(ang)
    b, s, _ = ang.shape
    rest = DA_QK_DIM - ROT_DIM
    cos64 = jnp.concatenate([cos, cos, jnp.ones((b, s, rest), F32)], axis=-1)
    sin64 = jnp.concatenate([sin, sin, jnp.zeros((b, s, rest), F32)], axis=-1)
    return jnp.tile(cos64, (1, 1, LANES // DA_QK_DIM)), jnp.tile(sin64, (1, 1, LANES // DA_QK_DIM))


def kernel(x, c, positions, norm_mix_g, norm_ffn_g, w_ada, b_ada, w_in, q_norm_g, k_norm_g, lam_q1, lam_k1, lam_q2, lam_k2, subln_g, conv_w, conv_b, dt_bias, a_log, d_skip, ssd_norm_g, w_out, peer_wq, peer_k1, peer_k2, peer_u, peer_v):
    depth = w_ada.shape[0]
    b, s, d = x.shape
    cos_t, sin_t = _rope_tables(positions)
    mod = adaln(c, w_ada, b_ada)
    for i in range(depth):
        lam_init = 0.8 - 0.6 * math.exp(-0.3 * i)
        sh_m, sc_m, g_m, sh_f, sc_f, g_f = [mod[i, :, j * d:(j + 1) * d] for j in range(6)]
        w_in_p = jnp.pad(w_in[i], ((0, 0), (0, IN_DIM_PAD - IN_DIM))).astype(BF16)
        proj = in_proj(x, norm_mix_g[i], sh_m, sc_m, w_in_p)
        qn, kn, vn = qk_prep(proj, cos_t, sin_t, q_norm_g[i], k_norm_g[i])
        lam = (jnp.exp(jnp.sum(lam_q1[i] * lam_k1[i])) - jnp.exp(jnp.sum(lam_q2[i] * lam_k2[i])) + lam_init)
        attn = diff_attention(qn, kn, vn, lam, subln_g[i], lam_init)
        ssd = ssd_group(proj, conv_w[i], conv_b[i], dt_bias[i], a_log[i], d_skip[i], ssd_norm_g[i])
        x = out_proj(attn, ssd, w_out[i].astype(BF16), x, g_m)
        h2, idx, gate = peer_route(x, norm_ffn_g[i], sh_f, sc_f, peer_wq[i].astype(BF16),
                                   peer_k1[i].astype(BF16), peer_k2[i].astype(BF16))
        n_exp = peer_u.shape[1]
        uv = jnp.concatenate([peer_u[i].astype(BF16).reshape(n_exp, _ROW_TILES, LANES),
                              peer_v[i].astype(BF16).reshape(n_exp, _ROW_TILES, LANES)], axis=1)
        y3 = peer_experts(idx.reshape(b * s, _NSEL), gate.reshape(b * s, _NSEL),
                          h2, uv)
        x = gated_residual(x, y3, g_f)
    return x
```
